```python
import math
import jax
import jax.numpy as jnp
from jax import lax
import numpy as np

D_MODEL = 1024
BATCH = 2
SEQ = 16384
DEPTH = 4

GRID_W = 64
CTX_LEN = 256
N_MIXERS = 4
N_GDN = (DEPTH + 3) // 4
N_RWKV = (DEPTH + 2) // 4
N_GLA = (DEPTH + 1) // 4
N_MLA = DEPTH // 4
DEEPNORM_ALPHA = (2.0 * DEPTH) ** 0.25
DEEPNORM_BETA = (8.0 * DEPTH) ** -0.25
CHUNK = 64
LN_EPS = 1e-5
RMS_EPS = 1e-6

GDN_HEADS = 8
GDN_DK = 128
GDN_DV = 128
GDN_CONV = 5
GDN_QK = GDN_HEADS * GDN_DK
GDN_V = GDN_HEADS * GDN_DV
GDN_IN = 2 * GDN_QK + 2 * GDN_V + 4 * GDN_HEADS

RWKV_HEAD = 64
RWKV_HEADS = D_MODEL // RWKV_HEAD
RWKV_DECAY_RANK = 64
RWKV_ICL_RANK = 64
RWKV_GN_EPS = 64e-5

GLA_HEADS = 4
GLA_DK = D_MODEL // 2 // GLA_HEADS
GLA_DV = D_MODEL // GLA_HEADS
GLA_GATE_RANK = 16
GLA_GATE_TAU = 16.0
GLA_QK = GLA_HEADS * GLA_DK
GLA_V = GLA_HEADS * GLA_DV
GLA_IN = 2 * GLA_QK + 2 * GLA_V + 2 * GLA_GATE_RANK

MLA_HEADS = 8
MLA_Q_LORA = 384
MLA_KV_LORA = 256
MLA_NOPE = 128
MLA_ROPE = 64
MLA_DV = 128
MLA_IN = MLA_Q_LORA + MLA_KV_LORA + MLA_ROPE + MLA_HEADS * MLA_DV
MLA_SCALE = (MLA_NOPE + MLA_ROPE) ** -0.5
Q_BLOCK = 128
ROPE_BASE = 10000.0

kernel_name = 'hybrid_bidir_gdn_rwkv7_gla_mla_dit'

F32 = jnp.float32


def layer_norm(x, g, b):
    xf = x.astype(F32)
    xc = xf - jnp.mean(xf, -1, keepdims=True)
    y = xc * lax.rsqrt(jnp.mean(xc * xc, -1, keepdims=True) + LN_EPS)
    return (y * g.astype(F32) + b.astype(F32)).astype(x.dtype)


def rms_norm(x, g):
    xf = x.astype(F32)
    y = xf * lax.rsqrt(jnp.mean(xf * xf, -1, keepdims=True) + RMS_EPS)
    return (y * g.astype(F32)).astype(x.dtype)


def l2_normalize(x):
    xf = x.astype(F32)
    return xf * lax.rsqrt(jnp.sum(xf * xf, -1, keepdims=True) + RMS_EPS)


def to_heads(t, n_heads):
    t = t.reshape(*t.shape[:-1], n_heads, -1)
    return jnp.swapaxes(t, -3, -2)


def from_heads(t):
    t = jnp.swapaxes(t, -3, -2)
    return t.reshape(*t.shape[:-2], -1)


def flip_seq(t):
    return jnp.flip(t, -2)


def centred_depthwise_conv(u, w):
    k = w.shape[0]
    return lax.conv_general_dilated(u, w[:, None, :].astype(u.dtype), window_strides=(1,),
                                    padding=[(k // 2, k // 2)],
                                    dimension_numbers=('NWC', 'WIO', 'NWC'),
                                    feature_group_count=u.shape[-1])


def centred_token_shift(u):
    prev = jnp.pad(u[:, :-1], ((0, 0), (1, 0), (0, 0)))
    nxt = jnp.pad(u[:, 1:], ((0, 0), (0, 1), (0, 0)))
    return 0.5 * (prev + nxt) - u


def axial_rope_tables(n_tokens):
    rows = n_tokens // GRID_W
    row = jnp.repeat(jnp.arange(rows, dtype=F32), GRID_W)
    col = jnp.tile(jnp.arange(GRID_W, dtype=F32), rows)
    n_freq = MLA_ROPE // 4
    inv_freq = ROPE_BASE ** (-jnp.arange(n_freq, dtype=F32) / n_freq)
    ang = jnp.concatenate([row[:, None] * inv_freq, col[:, None] * inv_freq], axis=-1)
    return jnp.cos(ang), jnp.sin(ang)


def apply_rope(t, cos, sin):
    t1, t2 = jnp.split(t, 2, axis=-1)
    cos = cos.astype(t.dtype)
    sin = sin.astype(t.dtype)
    return jnp.concatenate([t1 * cos - t2 * sin, t1 * sin + t2 * cos], axis=-1)


def gated_delta_rule(q, k, v, beta, g, state):
    b, h, l, dk = q.shape
    dv = v.shape[-1]
    n = l // CHUNK
    q, k, v = (t.reshape(b, h, n, CHUNK, -1) for t in (q, k, v))
    beta, g = (t.reshape(b, h, n, CHUNK) for t in (beta, g))
    gc = jnp.cumsum(g, axis=-1)
    causal = jnp.tril(jnp.ones((CHUNK, CHUNK), bool))
    strict = jnp.tril(jnp.ones((CHUNK, CHUNK), bool), -1)
    decay = jnp.exp(jnp.where(causal, gc[..., :, None] - gc[..., None, :], -jnp.inf))
    kb = k * beta[..., None]
    lower = jnp.where(strict, jnp.einsum('bhnid,bhnjd->bhnij', kb, k) * decay, 0.0)
    rhs = jnp.concatenate([v * beta[..., None], kb * jnp.exp(gc)[..., None]], axis=-1)
    sol = lax.linalg.triangular_solve(lower, rhs, left_side=True, lower=True, unit_diagonal=True)
    u0, wk = sol[..., :dv], sol[..., dv:]
    qk = jnp.einsum('bhnid,bhnjd->bhnij', q, k) * decay
    qg = q * jnp.exp(gc)[..., None]
    kd = k * jnp.exp(gc[..., -1:] - gc)[..., None]
    gl = jnp.exp(gc[..., -1])

    def step(s, inp):
        u0_c, wk_c, qk_c, qg_c, kd_c, gl_c = inp
        u = u0_c - jnp.einsum('bhck,bhkv->bhcv', wk_c, s)
        o = jnp.einsum('bhck,bhkv->bhcv', qg_c, s) + jnp.einsum('bhij,bhjv->bhiv', qk_c, u)
        s = s * gl_c[..., None, None] + jnp.einsum('bhck,bhcv->bhkv', kd_c, u)
        return s, o

    xs = tuple(jnp.moveaxis(t, 2, 0) for t in (u0, wk, qk, qg, kd, gl))
    state, o = lax.scan(step, state, xs)
    return jnp.moveaxis(o, 0, 2).reshape(b, h, l, dv), state


def gdn_mixer(h, hc, w_in, conv_w, a_log, dt_bias, norm_g, w_out, need_ctx):
    def project(u):
        p = u @ w_in
        qkv = jax.nn.silu(centred_depthwise_conv(p[..., :2 * GDN_QK + GDN_V], conv_w))
        q = l2_normalize(to_heads(qkv[..., :GDN_QK], GDN_HEADS)) * GDN_DK ** -0.5
        k = l2_normalize(to_heads(qkv[..., GDN_QK:2 * GDN_QK], GDN_HEADS))
        v = to_heads(qkv[..., 2 * GDN_QK:], GDN_HEADS).astype(F32)
        z = p[..., 2 * GDN_QK + GDN_V:2 * GDN_QK + 2 * GDN_V]
        ab = p[..., 2 * GDN_QK + 2 * GDN_V:].astype(F32)
        ab = ab.reshape(*ab.shape[:-1], 2, 2, GDN_HEADS)
        ab = jnp.swapaxes(jnp.moveaxis(ab, (2, 3), (0, 1)), -1, -2)
        beta = jax.nn.sigmoid(ab[:, 0])
        g = -jnp.exp(a_log.astype(F32))[:, None, :, None] * jax.nn.softplus(
            ab[:, 1] + dt_bias.astype(F32)[:, None, :, None])
        return q, k, v, z, beta, g

    def bidir(q, k, v, beta, g, s_f, s_b):
        o_f, s_f = gated_delta_rule(q, k, v, beta[0], g[0], s_f)
        o_b, s_b = gated_delta_rule(flip_seq(q), flip_seq(k), flip_seq(v),
                                    jnp.flip(beta[1], -1), jnp.flip(g[1], -1), s_b)
        return o_f + flip_seq(o_b), s_f, s_b

    def finish(o, z):
        o = rms_norm(jnp.swapaxes(o, 1, 2), norm_g)
        gate = jax.nn.silu(z.astype(F32)).reshape(o.shape)
        return (o * gate).reshape(*o.shape[:2], -1).astype(h.dtype) @ w_out

    zero = jnp.zeros((h.shape[0], GDN_HEADS, GDN_DK, GDN_DV), F32)
    qc, kc, vc, zc, bc, gcx = project(hc)
    oc, s_f, s_b = bidir(qc, kc, vc, bc, gcx, zero, zero)
    q, k, v, z, bt, g = project(h)
    o, _, _ = bidir(q, k, v, bt, g, s_f, s_b)
    y = finish(o, z)
    yc = finish(oc, zc) if need_ctx else None
    return y, yc


def rwkv7_scan(r, w, k, v, kk, kka, state):
    def step(s, inp):
        r_t, w_t, k_t, v_t, kk_t, kka_t = inp
        sa = jnp.einsum('bhvk,bhk->bhv', s, kk_t)
        s = s * w_t[:, :, None, :] - sa[..., None] * kka_t[:, :, None, :] + v_t[..., None] * k_t[:, :, None, :]
        return s, jnp.einsum('bhvk,bhk->bhv', s, r_t)

    xs = tuple(jnp.moveaxis(t, 2, 0) for t in (r, w, k, v, kk, kka))
    state, y = lax.scan(step, state, xs)
    return jnp.moveaxis(y, 0, 2), state


def rwkv7_mixer(h, hc, mu, w_rkvz, w0, w1, w2, a0, a1, a2, k_k, k_a, r_k, gn_g, gn_b, w_out, need_ctx):
    def project(u):
        xx = centred_token_shift(u)
        xr, xw, xk, xv, xa, xg = (u + xx * mu[i] for i in range(6))
        r = to_heads(xr @ w_rkvz[0], RWKV_HEADS).astype(F32)
        k = xk @ w_rkvz[1]
        v = to_heads(xv @ w_rkvz[2], RWKV_HEADS).astype(F32)
        z = xg @ w_rkvz[3]
        lw = jnp.tanh(jnp.einsum('bld,ndr->nblr', xw, w1))
        wlog = (w0[:, None, None, :] + jnp.einsum('nblr,nrd->nbld', lw, w2)).astype(F32)
        decay = jnp.exp(-jnp.exp(-jax.nn.softplus(-wlog) - 0.5))
        a = jax.nn.sigmoid((a0[:, None, None, :] + jnp.einsum(
            'nblr,nrd->nbld', jnp.einsum('bld,ndr->nblr', xa, a1), a2)).astype(F32))
        kk = l2_normalize(to_heads(k * k_k, RWKV_HEADS))
        k_dir = to_heads(k.astype(F32)[None] * (1.0 + (a - 1.0) * k_a.astype(F32)), RWKV_HEADS)
        kka = kk[None] * to_heads(a, RWKV_HEADS)
        return r, to_heads(decay, RWKV_HEADS), k_dir, v, kk, kka, z

    def bidir(r, w, kd, v, kk, kka, s_f, s_b):
        y_f, s_f = rwkv7_scan(r, w[0], kd[0], v, kk, kka[0], s_f)
        y_b, s_b = rwkv7_scan(*(flip_seq(t) for t in (r, w[1], kd[1], v, kk, kka[1])), s_b)
        return y_f + flip_seq(y_b), s_f, s_b

    def finish(y, r, kd, v, z):
        mean = jnp.mean(y, -1, keepdims=True)
        yc_ = y - mean
        var = jnp.mean(yc_ * yc_, -1, keepdims=True)
        g_ = gn_g.astype(F32).reshape(RWKV_HEADS, 1, RWKV_HEAD)
        b_ = gn_b.astype(F32).reshape(RWKV_HEADS, 1, RWKV_HEAD)
        yn = yc_ * lax.rsqrt(var + RWKV_GN_EPS) * g_ + b_
        bonus = jnp.sum(jnp.sum(r[None] * kd * r_k.astype(F32)[:, None, :], -1, keepdims=True), 0) * v
        out = from_heads(yn + bonus) * jax.nn.silu(z.astype(F32))
        return out.astype(h.dtype) @ w_out

    zero = jnp.zeros((h.shape[0], RWKV_HEADS, RWKV_HEAD, RWKV_HEAD), F32)
    rc, wc, kdc, vc, kkc, kkac, zc = project(hc)
    yc, s_f, s_b = bidir(rc, wc, kdc, vc, kkc, kkac, zero, zero)
    r, w, kd, v, kk, kka, z = project(h)
    y, _, _ = bidir(r, w, kd, v, kk, kka, s_f, s_b)
    out = finish(y, r, kd, v, z)
    out_c = finish(yc, rc, kdc, vc, zc) if need_ctx else None
    return out, out_c


def gla_chunked(q, k, v, g, state):
    b, h, l, dk = q.shape
    dv = v.shape[-1]
    n = l // CHUNK
    q, k, v, g = (t.reshape(b, h, n, CHUNK, -1) for t in (q, k, v, g))
    bc = jnp.cumsum(g, axis=-2)
    ref = bc[..., CHUNK // 2:CHUNK // 2 + 1, :]
    causal = jnp.tril(jnp.ones((CHUNK, CHUNK), bool))
    att = jnp.einsum('bhnid,bhnjd->bhnij', q * jnp.exp(bc - ref), k * jnp.exp(ref - bc))
    att = jnp.where(causal, att, 0.0)
    o_intra = jnp.einsum('bhnij,bhnjv->bhniv', att, v)
    qg = q * jnp.exp(bc)
    kd = k * jnp.exp(bc[..., -1:, :] - bc)
    gl = jnp.exp(bc[..., -1, :])

    def step(s, inp):
        qg_c, kd_c, v_c, gl_c = inp
        o = jnp.einsum('bhck,bhkv->bhcv', qg_c, s)
        s = s * gl_c[..., None] + jnp.einsum('bhck,bhcv->bhkv', kd_c, v_c)
        return s, o

    xs = tuple(jnp.moveaxis(t, 2, 0) for t in (qg, kd, v, gl))
    state, o_inter = lax.scan(step, state, xs)
    o = o_intra + jnp.moveaxis(o_inter, 0, 2)
    return o.reshape(b, h, l, dv), state


def gla_mixer(h, hc, w_in, w_g2, b_g, norm_g, w_out, need_ctx):
    def project(u):
        p = u @ w_in
        q = to_heads(p[..., :GLA_QK], GLA_HEADS).astype(F32) * GLA_DK ** -0.5
        k = to_heads(p[..., GLA_QK:2 * GLA_QK], GLA_HEADS).astype(F32)
        v = to_heads(p[..., 2 * GLA_QK:2 * GLA_QK + GLA_V], GLA_HEADS).astype(F32)
        z = p[..., 2 * GLA_QK + GLA_V:2 * GLA_QK + 2 * GLA_V]
        gr = p[..., 2 * GLA_QK + 2 * GLA_V:]
        gr = gr.reshape(*gr.shape[:-1], 2, GLA_GATE_RANK)
        glog = jax.nn.log_sigmoid((jnp.einsum('blnr,nrk->nblk', gr, w_g2)
                                   + b_g[:, None, None, :]).astype(F32)) / GLA_GATE_TAU
        return q, k, v, z, to_heads(glog, GLA_HEADS)

    def bidir(q, k, v, g, s_f, s_b):
        o_f, s_f = gla_chunked(q, k, v, g[0], s_f)
        o_b, s_b = gla_chunked(flip_seq(q), flip_seq(k), flip_seq(v), flip_seq(g[1]), s_b)
        return o_f + flip_seq(o_b), s_f, s_b

    def finish(o, z):
        o = rms_norm(jnp.swapaxes(o, 1, 2), norm_g)
        gate = jax.nn.silu(z.astype(F32)).reshape(o.shape)
        return (o * gate).reshape(*o.shape[:2], -1).astype(h.dtype) @ w_out

    zero = jnp.zeros((h.shape[0], GLA_HEADS, GLA_DK, GLA_DV), F32)
    qc, kc, vc, zc, gcx = project(hc)
    oc, s_f, s_b = bidir(qc, kc, vc, gcx, zero, zero)
    q, k, v, z, g = project(h)
    o, _, _ = bidir(q, k, v, g, s_f, s_b)
    y = finish(o, z)
    yc = finish(oc, zc) if need_ctx else None
    return y, yc


def mla_attend(qn, qr, kn, kr, v):
    s = (jnp.einsum('bqhd,bkhd->bhqk', qn, kn) + jnp.einsum('bqhd,bkd->bhqk', qr, kr)).astype(F32) * MLA_SCALE
    p = jax.nn.softmax(s, axis=-1)
    return jnp.einsum('bhqk,bkhd->bqhd', p.astype(v.dtype), v)


def mla_mixer(h, hc, w_in, q_norm, kv_norm, w_uq, w_ukv, w_out, cos, sin, need_ctx):
    def project(u, rotate):
        b, l, _ = u.shape
        p = u @ w_in
        o1 = MLA_Q_LORA
        o2 = o1 + MLA_KV_LORA
        o3 = o2 + MLA_ROPE
        q = (rms_norm(p[..., :o1], q_norm) @ w_uq).reshape(b, l, MLA_HEADS, MLA_NOPE + MLA_ROPE)
        kv = (rms_norm(p[..., o1:o2], kv_norm) @ w_ukv).reshape(b, l, MLA_HEADS, MLA_NOPE + MLA_DV)
        q_nope, q_rope = q[..., :MLA_NOPE], q[..., MLA_NOPE:]
        k_nope, v = kv[..., :MLA_NOPE], kv[..., MLA_NOPE:]
        k_rope = p[..., o2:o3]
        if rotate:
            q_rope = apply_rope(q_rope, cos[:, None, :], sin[:, None, :])
            k_rope = apply_rope(k_rope, cos, sin)
        return q_nope, q_rope, k_nope, k_rope, v, p[..., o3:]

    def finish(o, z):
        return (o * jax.nn.silu(z)) @ w_out

    b, l, _ = h.shape
    qn_c, qr_c, kn_c, kr_c, v_c, z_c = project(hc, False)
    qn, qr, kn, kr, v, z = project(h, True)
    kn_all = jnp.concatenate([kn, kn_c], axis=1)
    kr_all = jnp.concatenate([kr, kr_c], axis=1)
    v_all = jnp.concatenate([v, v_c], axis=1)
    nb = l // Q_BLOCK

    def blocks(t):
        return jnp.moveaxis(t.reshape(b, nb, Q_BLOCK, *t.shape[2:]), 1, 0)

    o = lax.map(lambda qb: mla_attend(qb[0], qb[1], kn_all, kr_all, v_all), (blocks(qn), blocks(qr)))
    o = jnp.moveaxis(o, 0, 1).reshape(b, l, MLA_HEADS * MLA_DV)
    y = finish(o, z)
    if need_ctx:
        oc = mla_attend(qn_c, qr_c, kn_c, kr_c, v_c).reshape(b, -1, MLA_HEADS * MLA_DV)
        return y, finish(oc, z_c)
    return y, None


def setup_inputs(seed: int = 0) -> dict:
    key = jax.random.key(seed)
    ks = list(jax.random.split(key, 48))
    d = D_MODEL

    def nrm(shape, scale):
        return scale * jax.random.normal(ks.pop(), shape, F32)

    def unif(shape, lo, hi):
        return jax.random.uniform(ks.pop(), shape, F32, lo, hi)

    def out_scale(fan_in):
        return DEEPNORM_BETA * fan_in ** -0.5

    dt = jnp.exp(unif((N_GDN, 2, GDN_HEADS), math.log(1e-3), math.log(1e-1)))
    return {
        'x': nrm((BATCH, SEQ, d), 1.0),
        'c': nrm((BATCH, d), 1.0),
        'ctx': nrm((BATCH, CTX_LEN, d), 1.0),
        'c_ctx': nrm((d,), 1.0),
        'ada_w': nrm((DEPTH, d, 3 * d), 0.5 * d ** -0.5),
        'ada_b': nrm((DEPTH, 3 * d), 0.02),
        'ln_g': 1.0 + nrm((DEPTH, d), 0.02),
        'ln_b': nrm((DEPTH, d), 0.02),
        'gdn_w_in': nrm((N_GDN, d, GDN_IN), d ** -0.5),
        'gdn_conv': nrm((N_GDN, GDN_CONV, 2 * GDN_QK + GDN_V), GDN_CONV ** -0.5),
        'gdn_a_log': jnp.log(unif((N_GDN, 2, GDN_HEADS), 1.0, 16.0)),
        'gdn_dt_bias': dt + jnp.log(-jnp.expm1(-dt)),
        'gdn_norm': 1.0 + nrm((N_GDN, GDN_DV), 0.02),
        'gdn_w_out': nrm((N_GDN, GDN_V, d), out_scale(GDN_V)),
        'rwkv_mu': unif((N_RWKV, 6, d), 0.0, 1.0),
        'rwkv_w_rkvz': nrm((N_RWKV, 4, d, d), d ** -0.5),
        'rwkv_w0': unif((N_RWKV, 2, d), -6.0, 0.0),
        'rwkv_w1': nrm((N_RWKV, 2, d, RWKV_DECAY_RANK), d ** -0.5),
        'rwkv_w2': nrm((N_RWKV, 2, RWKV_DECAY_RANK, d), 0.1 * RWKV_DECAY_RANK ** -0.5),
        'rwkv_a0': nrm((N_RWKV, 2, d), 0.1),
        'rwkv_a1': nrm((N_RWKV, 2, d, RWKV_ICL_RANK), d ** -0.5),
        'rwkv_a2': nrm((N_RWKV, 2, RWKV_ICL_RANK, d), 0.1 * RWKV_ICL_RANK ** -0.5),
        'rwkv_k_k': 0.85 + nrm((N_RWKV, d), 0.02),
        'rwkv_k_a': 1.0 + nrm((N_RWKV, d), 0.02),
        'rwkv_r_k': nrm((N_RWKV, RWKV_HEADS, RWKV_HEAD), 0.1),
        'rwkv_gn_g': 1.0 + nrm((N_RWKV, d), 0.02),
        'rwkv_gn_b': nrm((N_RWKV, d), 0.02),
        'rwkv_w_out': nrm((N_RWKV, d, d), out_scale(d)),
        'gla_w_in': nrm((N_GLA, d, GLA_IN), d ** -0.5),
        'gla_w_g2': nrm((N_GLA, 2, GLA_GATE_RANK, GLA_QK), GLA_GATE_RANK ** -0.5),
        'gla_b_g': unif((N_GLA, 2, GLA_QK), 0.0, 4.0),
        'gla_norm': 1.0 + nrm((N_GLA, GLA_DV), 0.02),
        'gla_w_out': nrm((N_GLA, GLA_V, d), out_scale(GLA_V)),
        'mla_w_in': nrm((N_MLA, d, MLA_IN), d ** -0.5),
        'mla_q_norm': 1.0 + nrm((N_MLA, MLA_Q_LORA), 0.02),
        'mla_kv_norm': 1.0 + nrm((N_MLA, MLA_KV_LORA), 0.02),
        'mla_w_uq': nrm((N_MLA, MLA_Q_LORA, MLA_HEADS * (MLA_NOPE + MLA_ROPE)), MLA_Q_LORA ** -0.5),
        'mla_w_ukv': nrm((N_MLA, MLA_KV_LORA, MLA_HEADS * (MLA_NOPE + MLA_DV)), MLA_KV_LORA ** -0.5),
        'mla_w_out': nrm((N_MLA, MLA_HEADS * MLA_DV, d), out_scale(MLA_HEADS * MLA_DV)),
    }


def reference(x, c, ctx, c_ctx, ada_w, ada_b, ln_g, ln_b,
              gdn_w_in, gdn_conv, gdn_a_log, gdn_dt_bias, gdn_norm, gdn_w_out,
              rwkv_mu, rwkv_w_rkvz, rwkv_w0, rwkv_w1, rwkv_w2, rwkv_a0, rwkv_a1, rwkv_a2,
              rwkv_k_k, rwkv_k_a, rwkv_r_k, rwkv_gn_g, rwkv_gn_b, rwkv_w_out,
              gla_w_in, gla_w_g2, gla_b_g, gla_norm, gla_w_out,
              mla_w_in, mla_q_norm, mla_kv_norm, mla_w_uq, mla_w_ukv, mla_w_out):
    cos, sin = axial_rope_tables(x.shape[1])
    xc = ctx
    silu_c = jax.nn.silu(c)
    silu_cc = jax.nn.silu(c_ctx)
    for i in range(DEPTH):
        kind, j = i % N_MIXERS, i // N_MIXERS
        need_ctx = i < DEPTH - 1
        shift, scale, gate = jnp.split(silu_c @ ada_w[i] + ada_b[i], 3, axis=-1)
        shift_c, scale_c, gate_c = jnp.split(silu_cc @ ada_w[i] + ada_b[i], 3, axis=-1)
        h = x * (1.0 + scale[:, None, :]) + shift[:, None, :]
        hc = xc * (1.0 + scale_c) + shift_c
        if kind == 0:
            y, yc = gdn_mixer(h, hc, gdn_w_in[j], gdn_conv[j], gdn_a_log[j], gdn_dt_bias[j],
                              gdn_norm[j], gdn_w_out[j], need_ctx)
        elif kind == 1:
            y, yc = rwkv7_mixer(h, hc, rwkv_mu[j], rwkv_w_rkvz[j], rwkv_w0[j], rwkv_w1[j], rwkv_w2[j],
                                rwkv_a0[j], rwkv_a1[j], rwkv_a2[j], rwkv_k_k[j], rwkv_k_a[j],
                                rwkv_r_k[j], rwkv_gn_g[j], rwkv_gn_b[j], rwkv_w_out[j], need_ctx)
        elif kind == 2:
            y, yc = gla_mixer(h, hc, gla_w_in[j], gla_w_g2[j], gla_b_g[j], gla_norm[j],
                              gla_w_out[j], need_ctx)
        else:
            y, yc = mla_mixer(h, hc, mla_w_in[j], mla_q_norm[j], mla_kv_norm[j], mla_w_uq[j],
                              mla_w_ukv[j], mla_w_out[j], cos, sin, need_ctx)
        x = layer_norm(DEEPNORM_ALPHA * x + gate[:, None, :] * y, ln_g[i], ln_b[i])
        if need_ctx:
            xc = layer_norm(DEEPNORM_ALPHA * xc + gate_c * yc, ln_g[i], ln_b[i])
    return x
```

```python
import functools
import math

import jax
import jax.numpy as jnp
from jax import lax
from jax.experimental import pallas as pl
from jax.experimental.pallas import tpu as pltpu

F32 = jnp.float32
BF16 = jnp.bfloat16
HI = lax.Precision.HIGHEST

LANES = 128
SUBLANES = 8
VMEM_LIMIT = 56 * 1024 * 1024

CHUNK = 64
SUB = 16
LN_EPS = 1e-5
RMS_EPS = 1e-6
GN_EPS = 64e-5
GLA_TAU = 16.0
ROPE_BASE = 10000.0
GRID_W = 64
ROW_TILE = 512
SCAN_TILE = 256


def _dot_hi(a, b):
    return jnp.dot(a, b, preferred_element_type=F32, precision=HI)


def _dot(a, b):
    return jnp.dot(a.astype(BF16), b.astype(BF16), preferred_element_type=F32)


def _bdot_hi(a, b):
    return lax.dot_general(a, b, (((2,), (1,)), ((0,), (0,))), preferred_element_type=F32, precision=HI)


def _bdot(a, b):
    return lax.dot_general(a.astype(BF16), b.astype(BF16), (((2,), (1,)), ((0,), (0,))),
                           preferred_element_type=F32)


def _bdot_nt(a, b):
    return lax.dot_general(a.astype(BF16), b.astype(BF16), (((2,), (2,)), ((0,), (0,))),
                           preferred_element_type=F32)


def _dot_nt(a, b):
    return lax.dot_general(a.astype(BF16), b.astype(BF16), (((1,), (1,)), ((), ())),
                           preferred_element_type=F32)


def _dot_tn(a, b):
    return lax.dot_general(a.astype(BF16), b.astype(BF16), (((0,), (0,)), ((), ())),
                           preferred_element_type=F32)


def _silu(x):
    return x * jax.nn.sigmoid(x)


def _softplus(x):
    return jnp.maximum(x, 0.0) + jnp.log(1.0 + jnp.exp(-jnp.abs(x)))


def _params(*sem):
    return pltpu.CompilerParams(dimension_semantics=sem, vmem_limit_bytes=VMEM_LIMIT)


def _row_tile(n, cap):
    t = min(n, cap)
    assert n % t == 0 and t % SUBLANES == 0, (n, t)
    return t


def _col_tile(n, cap=768):
    assert n % LANES == 0, n
    best = LANES
    for t in range(LANES, cap + 1, LANES):
        if n % t == 0:
            best = t
    return best


def _masks(reverse, nc):
    r = lax.broadcasted_iota(jnp.int32, (nc, CHUNK, CHUNK), 1)
    c = lax.broadcasted_iota(jnp.int32, (nc, CHUNK, CHUNK), 2)
    if reverse:
        incl, strict = c >= r, c > r
    else:
        incl, strict = c <= r, c < r
    bd = (r // SUB) == (c // SUB)
    eye = (r == c).astype(F32)
    return incl, strict, bd, eye


def _tri_inverse(lm, bd, eye):
    d = jnp.where(bd, lm, 0.0)
    e = lm - d
    d2 = _bdot_hi(d, d)
    d4 = _bdot_hi(d2, d2)
    d8 = _bdot_hi(d4, d4)
    t16 = _bdot_hi(_bdot_hi(eye - d, eye + d2), _bdot_hi(eye + d4, eye + d8))
    f = _bdot_hi(t16, e)
    f2 = _bdot_hi(f, f)
    return _bdot_hi(_bdot_hi(eye - f, eye + f2), t16)


def _chunks(x):
    return x.reshape(x.shape[0] // CHUNK, CHUNK, x.shape[1])


def _chunk_order(nc, reverse):
    return range(nc - 1, -1, -1) if reverse else range(nc)


def _modulate(x, scale, shift):
    return x * (1.0 + scale) + shift


def _ada_kernel(c_ref, w_ref, b_ref, o_ref):
    o_ref[0] = _dot_hi(_silu(c_ref[...]), w_ref[0]) + b_ref[0]


def _ada(cvec, ada_w, ada_b):
    depth, d, n = ada_w.shape
    tn = _col_tile(n, 512)
    rows = cvec.shape[0]
    return pl.pallas_call(
        _ada_kernel,
        grid=(depth, n // tn),
        in_specs=[pl.BlockSpec((rows, d), lambda i, j: (0, 0)),
                  pl.BlockSpec((1, d, tn), lambda i, j: (i, 0, j)),
                  pl.BlockSpec((1, 1, tn), lambda i, j: (i, 0, j))],
        out_specs=pl.BlockSpec((1, rows, tn), lambda i, j: (i, 0, j)),
        out_shape=jax.ShapeDtypeStruct((depth, rows, n), F32),
        compiler_params=_params("arbitrary", "arbitrary"),
        name="ada",
    )(cvec, ada_w, ada_b.reshape(depth, 1, n))


def _proj_kernel(x_ref, sc_ref, sh_ref, w_ref, o_ref, *, tn):
    h = _modulate(x_ref[0], sc_ref[0], sh_ref[0]).astype(BF16)
    for j in range(w_ref.shape[1] // tn):
        o_ref[0, :, j * tn:(j + 1) * tn] = jnp.dot(h, w_ref[:, j * tn:(j + 1) * tn],
                                                   preferred_element_type=F32)


def _proj(x, scale, shift, w):
    b, l, d = x.shape
    n = w.shape[1]
    tm = _row_tile(l, ROW_TILE)
    vec = pl.BlockSpec((1, 1, d), lambda bi, i: (bi, 0, 0))
    return pl.pallas_call(
        functools.partial(_proj_kernel, tn=_col_tile(n)),
        grid=(b, l // tm),
        in_specs=[pl.BlockSpec((1, tm, d), lambda bi, i: (bi, i, 0)), vec, vec,
                  pl.BlockSpec((d, n), lambda bi, i: (0, 0))],
        out_specs=pl.BlockSpec((1, tm, n), lambda bi, i: (bi, i, 0)),
        out_shape=jax.ShapeDtypeStruct((b, l, n), F32),
        compiler_params=_params("arbitrary", "arbitrary"),
        name="proj",
    )(x, scale, shift, w)


def _group_rms(o, g, width):
    parts = []
    for j in range(o.shape[1] // width):
        oj = o[:, j * width:(j + 1) * width]
        parts.append(oj * lax.rsqrt(jnp.mean(oj * oj, -1, keepdims=True) + RMS_EPS) * g)
    return jnp.concatenate(parts, axis=1)


def _out_tail(pre, x_ref, gate_ref, w_ref, lg_ref, lb_ref, o_ref, alpha):
    y = jnp.dot(pre.astype(BF16), w_ref[...], preferred_element_type=F32)
    r = alpha * x_ref[0] + gate_ref[0] * y
    rc = r - jnp.mean(r, -1, keepdims=True)
    o_ref[0] = rc * lax.rsqrt(jnp.mean(rc * rc, -1, keepdims=True) + LN_EPS) * lg_ref[...] + lb_ref[...]


def _out_rms_kernel(of_ref, ob_ref, z_ref, g_ref, x_ref, gate_ref, w_ref, lg_ref, lb_ref, o_ref, *, alpha, width):
    o = _group_rms(of_ref[0] + ob_ref[0], g_ref[...], width)
    _out_tail(o * _silu(z_ref[0]), x_ref, gate_ref, w_ref, lg_ref, lb_ref, o_ref, alpha)


def _out_rwkv_kernel(yf_ref, yb_ref, bf_ref, bb_ref, z_ref, g_ref, gb_ref, x_ref, gate_ref, w_ref, lg_ref, lb_ref,
                     o_ref, *, alpha, width):
    y = yf_ref[0] + yb_ref[0]
    parts = []
    for j in range(y.shape[1] // width):
        yj = y[:, j * width:(j + 1) * width]
        yc = yj - jnp.mean(yj, -1, keepdims=True)
        parts.append(yc * lax.rsqrt(jnp.mean(yc * yc, -1, keepdims=True) + GN_EPS))
    yn = jnp.concatenate(parts, axis=1) * g_ref[...] + gb_ref[...]
    pre = (yn + bf_ref[0] + bb_ref[0]) * _silu(z_ref[0])
    _out_tail(pre, x_ref, gate_ref, w_ref, lg_ref, lb_ref, o_ref, alpha)


def _out_mla_kernel(o_in_ref, z_ref, x_ref, gate_ref, w_ref, lg_ref, lb_ref, o_ref, *, alpha):
    _out_tail(o_in_ref[0] * _silu(z_ref[0]), x_ref, gate_ref, w_ref, lg_ref, lb_ref, o_ref, alpha)


def _out(kind, acts, act_specs, vecs, x, gate, w_out, ln_g, ln_b, alpha, width=None):
    b, l, d = x.shape
    tm = _row_tile(l, ROW_TILE)
    k = w_out.shape[0]
    body = {"rms": functools.partial(_out_rms_kernel, alpha=alpha, width=width),
            "rwkv": functools.partial(_out_rwkv_kernel, alpha=alpha, width=width),
            "mla": functools.partial(_out_mla_kernel, alpha=alpha)}[kind]
    row = lambda n: pl.BlockSpec((1, n), lambda bi, i: (0, 0))
    in_specs = (list(act_specs(tm)) + [row(v.shape[1]) for v in vecs]
                + [pl.BlockSpec((1, tm, d), lambda bi, i: (bi, i, 0)),
                   pl.BlockSpec((1, 1, d), lambda bi, i: (bi, 0, 0)),
                   pl.BlockSpec((k, d), lambda bi, i: (0, 0)), row(d), row(d)])
    return pl.pallas_call(
        body,
        grid=(b, l // tm),
        in_specs=in_specs,
        out_specs=pl.BlockSpec((1, tm, d), lambda bi, i: (bi, i, 0)),
        out_shape=jax.ShapeDtypeStruct((b, l, d), F32),
        compiler_params=_params("arbitrary", "arbitrary"),
        name="out_" + kind,
    )(*acts, *vecs, x, gate, w_out, ln_g.reshape(1, d), ln_b.reshape(1, d))


def _cols(tm, width, col):
    return pl.BlockSpec((1, tm, width), lambda bi, i: (bi, i, col))


def _gdn_kernel(q_ref, k_ref, v_ref, hp_q, hp_k, hp_v, hn_q, hn_k, hn_v, cw_q, cw_k, cw_v, ab_ref, al_ref, dt_ref,
                s0_ref, o_ref, sf_ref, ext_ref, s_ref, *, reverse, direction, heads, taps):
    h = pl.program_id(1)
    i = pl.program_id(2)
    tb = q_ref.shape[1]
    nc = tb // CHUNK
    pad = taps // 2

    @pl.when(i == 0)
    def _():
        s_ref[...] = s0_ref[0, 0]

    def conv(x_ref, hp_ref, hn_ref, cw_ref, slot):
        ext_ref[slot, 0:SUBLANES, :] = hp_ref[0, 0]
        ext_ref[slot, SUBLANES:SUBLANES + tb, :] = x_ref[0]
        ext_ref[slot, SUBLANES + tb:, :] = hn_ref[0, 0]
        acc = jnp.zeros((tb, LANES), F32)
        for t in range(taps):
            acc = acc + ext_ref[slot, pl.ds(SUBLANES - pad + t, tb), :] * cw_ref[t:t + 1, :]
        return _silu(acc)

    q = conv(q_ref, hp_q, hn_q, cw_q, 0)
    k = conv(k_ref, hp_k, hn_k, cw_k, 1)
    v = conv(v_ref, hp_v, hn_v, cw_v, 2)
    q = q * lax.rsqrt(jnp.sum(q * q, -1, keepdims=True) + RMS_EPS) * (LANES ** -0.5)
    k = k * lax.rsqrt(jnp.sum(k * k, -1, keepdims=True) + RMS_EPS)

    ab = ab_ref[0]
    lane = lax.broadcasted_iota(jnp.int32, (1, LANES), 1)
    col = direction * 2 * heads + h
    beta = jax.nn.sigmoid(jnp.sum(jnp.where(lane == col, ab, 0.0), -1, keepdims=True))
    gpre = jnp.sum(jnp.where(lane == col + heads, ab, 0.0), -1, keepdims=True)
    g_b = -jnp.exp(al_ref[0]) * _softplus(gpre + dt_ref[0])

    incl, strict, bd, eye = _masks(reverse, nc)
    q, k, v, g_b = _chunks(q), _chunks(k), _chunks(v), _chunks(g_b)
    beta = beta.reshape(nc, CHUNK, 1)
    tri = incl.astype(F32)
    gc_b = _bdot_hi(tri, g_b)
    gtot = jnp.sum(g_b, axis=1, keepdims=True)
    m = _bdot_hi(tri, jnp.where(strict, g_b[:, :, :CHUNK], 0.0))
    decay = jnp.where(incl, jnp.exp(m), 0.0)
    kb = k * beta
    lm = jnp.where(strict, _bdot_nt(kb, k) * decay, 0.0)
    t = _tri_inverse(lm, bd, eye)
    eg = jnp.exp(gc_b)
    u0 = _bdot(t, v * beta)
    wk = _bdot(t, kb * eg)
    qk = _bdot_nt(q, k) * decay
    qg = q * eg
    kd = k * jnp.exp(gtot - gc_b)
    gl = jnp.exp(gtot)

    s = s_ref[...]
    for c in _chunk_order(nc, reverse):
        u = u0[c] - _dot(wk[c], s)
        o_ref[0, c * CHUNK:(c + 1) * CHUNK, :] = _dot(qg[c], s) + _dot(qk[c], u)
        s = s * gl[c] + _dot_tn(kd[c], u)
    s_ref[...] = s
    sf_ref[0, 0] = s


def _gdn_core(p, halo_prev, halo_next, conv_w, a_log, dt_bias, s0, *, reverse, direction, heads):
    b, l, _ = p.shape
    tb = _row_tile(l, SCAN_TILE)
    nblk = l // tb
    taps = conv_w.shape[0]
    conv_w = jnp.pad(conv_w, ((0, SUBLANES - taps), (0, 0)))
    blk = (lambda i: nblk - 1 - i) if reverse else (lambda i: i)
    al = jnp.broadcast_to(a_log[direction][:, None, None], (heads, 1, LANES))
    dt = jnp.broadcast_to(dt_bias[direction][:, None, None], (heads, 1, LANES))

    def act(off):
        return pl.BlockSpec((1, tb, LANES), lambda bi, h, i: (bi, blk(i), off + h))

    def halo(off):
        return pl.BlockSpec((1, 1, SUBLANES, LANES), lambda bi, h, i: (bi, blk(i), 0, off + h))

    def cw(off):
        return pl.BlockSpec((SUBLANES, LANES), lambda bi, h, i: (0, off + h))

    per_head = pl.BlockSpec((1, 1, LANES), lambda bi, h, i: (h, 0, 0))
    state = pl.BlockSpec((1, 1, LANES, LANES), lambda bi, h, i: (bi, h, 0, 0))
    ab_col = (3 * heads * LANES + heads * LANES) // LANES
    return pl.pallas_call(
        functools.partial(_gdn_kernel, reverse=reverse, direction=direction, heads=heads, taps=taps),
        grid=(b, heads, nblk),
        in_specs=[act(0), act(heads), act(2 * heads),
                  halo(0), halo(heads), halo(2 * heads), halo(0), halo(heads), halo(2 * heads),
                  cw(0), cw(heads), cw(2 * heads),
                  pl.BlockSpec((1, tb, LANES), lambda bi, h, i: (bi, blk(i), ab_col)),
                  per_head, per_head, state],
        out_specs=[pl.BlockSpec((1, tb, LANES), lambda bi, h, i: (bi, blk(i), h)), state],
        out_shape=[jax.ShapeDtypeStruct((b, l, heads * LANES), F32),
                   jax.ShapeDtypeStruct(s0.shape, F32)],
        scratch_shapes=[pltpu.VMEM((3, tb + 2 * SUBLANES, LANES), F32), pltpu.VMEM((LANES, LANES), F32)],
        compiler_params=_params("arbitrary", "arbitrary", "arbitrary"),
        name="gdn_bwd" if reverse else "gdn_fwd",
    )(p, p, p, halo_prev, halo_prev, halo_prev, halo_next, halo_next, halo_next, conv_w, conv_w, conv_w, p, al, dt, s0)


def _conv_halos(p, width, tb, pad):
    b, l, _ = p.shape
    nblk = l // tb
    p5 = p[:, :, :width].reshape(b, nblk, tb, width)
    zeros = jnp.zeros((b, 1, pad, width), p.dtype)
    prev = jnp.concatenate([zeros, p5[:, :-1, tb - pad:, :]], axis=1)
    nxt = jnp.concatenate([p5[:, 1:, :pad, :], zeros], axis=1)
    prev = jnp.pad(prev, ((0, 0), (0, 0), (SUBLANES - pad, 0), (0, 0)))
    nxt = jnp.pad(nxt, ((0, 0), (0, 0), (0, SUBLANES - pad), (0, 0)))
    return prev, nxt


def _gla_kernel(q_ref, k_ref, v_ref, gr_ref, wg_ref, bg_ref, s0_ref, o_ref, sf_ref, s_ref, *, reverse):
    i = pl.program_id(2)
    tb = q_ref.shape[1]
    nc = tb // CHUNK

    @pl.when(i == 0)
    def _():
        s_ref[...] = s0_ref[0, 0]

    q = _chunks(q_ref[0] * (LANES ** -0.5))
    k = _chunks(k_ref[0])
    v = _chunks(v_ref[0])
    gpre = _dot_hi(gr_ref[0], wg_ref[0]) + bg_ref[0]
    g = _chunks(-_softplus(-gpre) / GLA_TAU)

    incl, _, _, _ = _masks(reverse, nc)
    bc = _bdot_hi(incl.astype(F32), g)
    btot = jnp.sum(g, axis=1, keepdims=True)
    mid = CHUNK - 1 - CHUNK // 2 if reverse else CHUNK // 2
    ref = bc[:, mid:mid + 1, :]
    att = jnp.where(incl, _bdot_nt(q * jnp.exp(bc - ref), k * jnp.exp(ref - bc)), 0.0)
    o_intra = _bdot(att, v)
    qg = q * jnp.exp(bc)
    kd = k * jnp.exp(btot - bc)
    gl = jnp.exp(btot)

    st = s_ref[...]
    for c in _chunk_order(nc, reverse):
        o_ref[0, c * CHUNK:(c + 1) * CHUNK, :] = o_intra[c] + _dot_nt(qg[c], st)
        st = st * gl[c] + _dot_tn(v[c], kd[c])
    s_ref[...] = st
    sf_ref[0, 0] = st


def _gla_core(p, wg, bg, s0, *, reverse, heads):
    b, l, _ = p.shape
    tb = _row_tile(l, SCAN_TILE)
    nblk = l // tb
    dk, dv = LANES, 2 * LANES
    blk = (lambda i: nblk - 1 - i) if reverse else (lambda i: i)
    gr_col = (2 * heads * dk + 2 * heads * dv) // LANES
    state = pl.BlockSpec((1, 1, dv, dk), lambda bi, h, i: (bi, h, 0, 0))
    return pl.pallas_call(
        functools.partial(_gla_kernel, reverse=reverse),
        grid=(b, heads, nblk),
        in_specs=[pl.BlockSpec((1, tb, dk), lambda bi, h, i: (bi, blk(i), h)),
                  pl.BlockSpec((1, tb, dk), lambda bi, h, i: (bi, blk(i), heads + h)),
                  pl.BlockSpec((1, tb, dv), lambda bi, h, i: (bi, blk(i), heads + h)),
                  pl.BlockSpec((1, tb, LANES), lambda bi, h, i: (bi, blk(i), gr_col)),
                  pl.BlockSpec((1, LANES, dk), lambda bi, h, i: (h, 0, 0)),
                  pl.BlockSpec((1, 1, dk), lambda bi, h, i: (h, 0, 0)),
                  state],
        out_specs=[pl.BlockSpec((1, tb, dv), lambda bi, h, i: (bi, blk(i), h)), state],
        out_shape=[jax.ShapeDtypeStruct((b, l, heads * dv), F32), jax.ShapeDtypeStruct(s0.shape, F32)],
        scratch_shapes=[pltpu.VMEM((dv, dk), F32)],
        compiler_params=_params("arbitrary", "arbitrary", "arbitrary"),
        name="gla_bwd" if reverse else "gla_fwd",
    )(p, p, p, p, wg, bg, s0)


def _rwkv_proj_kernel(x_ref, xp_ref, xn_ref, sc_ref, sh_ref, mu_ref, w4_ref, w1_ref, a1_ref, w2_ref, a2_ref,
                      w0_ref, a0_ref, r_ref, k_ref, v_ref, z_ref, lw_ref, a_ref):
    i = pl.program_id(1)
    last = pl.num_programs(1) - 1
    tm = x_ref.shape[1]
    rank = w2_ref.shape[1]
    sc, sh = sc_ref[0], sh_ref[0]
    u = _modulate(x_ref[0], sc, sh)
    up = jnp.where(i > 0, _modulate(xp_ref[0, 0, SUBLANES - 1:SUBLANES, :], sc, sh), 0.0)
    un = jnp.where(i < last, _modulate(xn_ref[0, 0, 0:1, :], sc, sh), 0.0)
    row = lax.broadcasted_iota(jnp.int32, (tm, 1), 0)
    prev = jnp.where(row == 0, up, pltpu.roll(u, 1, 0))
    nxt = jnp.where(row == tm - 1, un, pltpu.roll(u, tm - 1, 0))
    xx = 0.5 * (prev + nxt) - u

    def mix(j):
        return (u + xx * mu_ref[j:j + 1, :]).astype(BF16)

    r_ref[0] = jnp.dot(mix(0), w4_ref[0], preferred_element_type=F32)
    k_ref[0] = jnp.dot(mix(2), w4_ref[1], preferred_element_type=F32)
    v_ref[0] = jnp.dot(mix(3), w4_ref[2], preferred_element_type=F32)
    z_ref[0] = jnp.dot(mix(5), w4_ref[3], preferred_element_type=F32)
    lw1 = jnp.tanh(jnp.dot(mix(1), w1_ref[...], preferred_element_type=F32))
    a1 = jnp.dot(mix(4), a1_ref[...], preferred_element_type=F32)
    for n in range(2):
        wlog = w0_ref[n:n + 1, :] + _dot(lw1[:, n * rank:(n + 1) * rank], w2_ref[n])
        lw_ref[n, 0] = -jnp.exp(-_softplus(-wlog) - 0.5)
        a_ref[n, 0] = jax.nn.sigmoid(a0_ref[n:n + 1, :] + _dot(a1[:, n * rank:(n + 1) * rank], a2_ref[n]))


def _rwkv_proj(x, scale, shift, mu, w_rkvz, w0, w1, w2, a0, a1, a2):
    b, l, d = x.shape
    tm = _row_tile(l, 256)
    nblk = l // tm
    rank = w1.shape[2]
    x4 = x.reshape(b, nblk, tm, d)
    zeros = jnp.zeros((b, 1, SUBLANES, d), x.dtype)
    xp = jnp.concatenate([zeros, x4[:, :-1, tm - SUBLANES:, :]], axis=1)
    xn = jnp.concatenate([x4[:, 1:, :SUBLANES, :], zeros], axis=1)
    w1c = jnp.concatenate([w1[0], w1[1]], axis=1).astype(BF16)
    a1c = jnp.concatenate([a1[0], a1[1]], axis=1).astype(BF16)
    vec = pl.BlockSpec((1, 1, d), lambda bi, i: (bi, 0, 0))
    full = lambda a: pl.BlockSpec(a.shape, lambda bi, i: (0,) * a.ndim)
    halo = pl.BlockSpec((1, 1, SUBLANES, d), lambda bi, i: (bi, i, 0, 0))
    act = pl.BlockSpec((1, tm, d), lambda bi, i: (bi, i, 0))
    act2 = pl.BlockSpec((2, 1, tm, d), lambda bi, i: (0, bi, i, 0))
    w4 = w_rkvz.astype(BF16)
    w2b, a2b = w2.astype(BF16), a2.astype(BF16)
    sds = jax.ShapeDtypeStruct((b, l, d), F32)
    sds2 = jax.ShapeDtypeStruct((2, b, l, d), F32)
    return pl.pallas_call(
        _rwkv_proj_kernel,
        grid=(b, nblk),
        in_specs=[act, halo, halo, vec, vec, full(mu), full(w4), full(w1c), full(a1c), full(w2b), full(a2b),
                  full(w0), full(a0)],
        out_specs=[act, act, act, act, act2, act2],
        out_shape=[sds, sds, sds, sds, sds2, sds2],
        compiler_params=_params("arbitrary", "arbitrary"),
        name="rwkv_proj",
    )(x, xp, xn, scale, shift, mu, w4, w1c, a1c, w2b, a2b, w0, a0)


def _rwkv_kernel(r_ref, k_ref, v_ref, lw_ref, a_ref, kk_ref, ka_ref, rk_ref, s0_ref, y_ref, bv_ref, sf_ref, s_ref,
                 *, reverse, n):
    i = pl.program_id(2)
    tb = r_ref.shape[1]
    nc = tb // CHUNK
    per = LANES // n

    @pl.when(i == 0)
    def _():
        s_ref[...] = s0_ref[0]

    r, kr, v, lw, a = r_ref[0], k_ref[0], v_ref[0], lw_ref[0, 0], a_ref[0, 0]
    kq = kr * kk_ref[...]
    kd = kr * (1.0 + (a - 1.0) * ka_ref[...])
    rkd = r * kd * rk_ref[...]
    incl, strict, bd, eye = _masks(reverse, nc)
    tri = incl.astype(F32)
    gc = _bdot_hi(tri, _chunks(lw))
    lwc = _chunks(lw)
    gtot = jnp.sum(lwc, axis=1, keepdims=True)
    gcx = gc - lwc
    ref = 0.5 * gtot
    e_in = jnp.exp(ref - gc)
    e_out = jnp.exp(gtot - gc)
    e_gc = jnp.exp(gc)
    e_gcx = jnp.exp(gcx)
    e_a = jnp.exp(gcx - ref)
    e_r = jnp.exp(gc - ref)
    gl = jnp.exp(gtot)

    for hh in range(per):
        sl = slice(hh * n, (hh + 1) * n)
        kqh = kq[:, sl]
        kk = kqh * lax.rsqrt(jnp.sum(kqh * kqh, -1, keepdims=True) + RMS_EPS)
        ah, kdh, rh, vh = a[:, sl], kd[:, sl], r[:, sl], v[:, sl]
        bv_ref[0, :, sl] = jnp.sum(rkd[:, sl], -1, keepdims=True) * vh
        kka = kk * ah
        kk, kka, kdh, rh, vh = _chunks(kk), _chunks(kka), _chunks(kdh), _chunks(rh), _chunks(vh)
        at = kk * e_a[:, :, sl]
        bt = kka * e_in[:, :, sl]
        kt = kdh * e_in[:, :, sl]
        rt = rh * e_r[:, :, sl]
        a_ab = jnp.where(strict, _bdot_nt(at, bt), 0.0)
        a_ak = jnp.where(strict, _bdot_nt(at, kt), 0.0)
        a_rb = jnp.where(incl, _bdot_nt(rt, bt), 0.0)
        a_rk = jnp.where(incl, _bdot_nt(rt, kt), 0.0)
        t = _tri_inverse(a_ab, bd, eye)
        u0 = -_bdot(t, _bdot(a_ak, vh))
        w = _bdot(t, kk * e_gcx[:, :, sl])
        y0 = _bdot(a_rk, vh)
        r0 = rh * e_gc[:, :, sl]
        bh = kka * e_out[:, :, sl]
        kh = kdh * e_out[:, :, sl]

        st = s_ref[hh]
        for c in _chunk_order(nc, reverse):
            u = u0[c] - _dot_nt(w[c], st)
            y_ref[0, c * CHUNK:(c + 1) * CHUNK, sl] = _dot_nt(r0[c], st) + _dot(a_rb[c], u) + y0[c]
            st = st * gl[c][:, sl] + _dot_tn(u, bh[c]) + _dot_tn(vh[c], kh[c])
        s_ref[hh] = st
        sf_ref[0, hh] = st


def _rwkv_core(r, k, v, lw, a, k_k, k_a, r_k, s0, *, reverse, direction):
    b, l, d = r.shape
    n = s0.shape[-1]
    per = LANES // n
    tb = _row_tile(l, SCAN_TILE)
    nblk = l // tb
    blk = (lambda i: nblk - 1 - i) if reverse else (lambda i: i)
    act = pl.BlockSpec((1, tb, LANES), lambda bi, h, i: (bi, blk(i), h))
    act2 = pl.BlockSpec((1, 1, tb, LANES), lambda bi, h, i: (direction, bi, blk(i), h))
    vec = pl.BlockSpec((1, LANES), lambda bi, h, i: (0, h))
    state = pl.BlockSpec((1, per, n, n), lambda bi, h, i: (bi, h, 0, 0))
    sds = jax.ShapeDtypeStruct((b, l, d), F32)
    return pl.pallas_call(
        functools.partial(_rwkv_kernel, reverse=reverse, n=n),
        grid=(b, d // LANES, nblk),
        in_specs=[act, act, act, act2, act2, vec, vec, vec, state],
        out_specs=[act, act, state],
        out_shape=[sds, sds, jax.ShapeDtypeStruct(s0.shape, F32)],
        scratch_shapes=[pltpu.VMEM((per, n, n), F32)],
        compiler_params=_params("arbitrary", "arbitrary", "arbitrary"),
        name="rwkv_bwd" if reverse else "rwkv_fwd",
    )(r, k, v, lw, a, k_k.reshape(1, d), k_a.reshape(1, d), r_k[direction].reshape(1, d), s0)


def _mla_proj_kernel(x_ref, sc_ref, sh_ref, win_ref, qn_ref, kvn_ref, wq_ref, wk_ref, wv_ref, cq_ref, s1_ref, s2_ref,
                     q_ref, k_ref, v_ref, z_ref, *, heads, q_lora, kv_lora, scale):
    d = z_ref.shape[2]
    h = _modulate(x_ref[0], sc_ref[0], sh_ref[0]).astype(BF16)
    p = jnp.dot(h, win_ref[...], preferred_element_type=F32)

    def rms(t, g):
        return t * lax.rsqrt(jnp.mean(t * t, -1, keepdims=True) + RMS_EPS) * g

    def rope(t):
        return (t * cq_ref[...] + pltpu.roll(t, LANES - LANES // 4, 1) * s1_ref[...]
                + pltpu.roll(t, LANES // 4, 1) * s2_ref[...])

    ql = rms(p[:, :q_lora], qn_ref[...]).astype(BF16)
    kvl = rms(p[:, q_lora:q_lora + kv_lora], kvn_ref[...]).astype(BF16)
    z_ref[0] = p[:, q_lora + kv_lora:q_lora + kv_lora + d]
    kr = rope(p[:, q_lora + kv_lora + d:])
    q = jnp.dot(ql, wq_ref[...], preferred_element_type=F32) * scale
    kn = jnp.dot(kvl, wk_ref[...], preferred_element_type=F32)
    v_ref[0] = jnp.dot(kvl, wv_ref[...], preferred_element_type=F32).astype(BF16)
    for j in range(heads):
        base = 2 * LANES * j
        q_ref[0, :, base:base + LANES] = q[:, base:base + LANES].astype(BF16)
        q_ref[0, :, base + LANES:base + 2 * LANES] = rope(q[:, base + LANES:base + 2 * LANES]).astype(BF16)
        k_ref[0, :, base:base + LANES] = kn[:, j * LANES:(j + 1) * LANES].astype(BF16)
        k_ref[0, :, base + LANES:base + 2 * LANES] = kr.astype(BF16)


def _mla_proj(x, scale, shift, w_in, q_norm, kv_norm, w_uq, w_ukv, cos, sin, *, heads, nope, rope, dv):
    b, l, d = x.shape
    q_lora, kv_lora = q_norm.shape[0], kv_norm.shape[0]
    tm = _row_tile(l, 256)
    half = rope // 2
    assert nope == LANES and dv == LANES and rope == LANES // 2
    o2 = q_lora + kv_lora
    w_in_r = jnp.concatenate([w_in[:, :o2], w_in[:, o2 + rope:], w_in[:, o2:o2 + rope],
                              jnp.zeros((d, LANES - rope), w_in.dtype)], axis=1).astype(BF16)
    wq = w_uq.reshape(q_lora, heads, nope + rope)
    wq = jnp.concatenate([wq, jnp.zeros((q_lora, heads, LANES - rope), w_uq.dtype)], axis=2)
    wq = wq.reshape(q_lora, heads * 2 * LANES).astype(BF16)
    wkv = w_ukv.reshape(kv_lora, heads, nope + dv)
    wk = wkv[:, :, :nope].reshape(kv_lora, heads * nope).astype(BF16)
    wv = wkv[:, :, nope:].reshape(kv_lora, heads * dv).astype(BF16)
    zer = jnp.zeros((l, half), F32)
    zer2 = jnp.zeros((l, LANES - rope), F32)
    cq = jnp.concatenate([cos, cos, zer2], axis=1)
    s1 = jnp.concatenate([-sin, zer, zer2], axis=1)
    s2 = jnp.concatenate([zer, sin, zer2], axis=1)
    vec = pl.BlockSpec((1, 1, d), lambda bi, i: (bi, 0, 0))
    full = lambda a: pl.BlockSpec(a.shape, lambda bi, i: (0,) * a.ndim)
    tab = pl.BlockSpec((tm, LANES), lambda bi, i: (i, 0))
    row = lambda w: pl.BlockSpec((1, tm, w), lambda bi, i: (bi, i, 0))
    qn = q_norm.reshape(1, q_lora)
    kvn = kv_norm.reshape(1, kv_lora)
    return pl.pallas_call(
        functools.partial(_mla_proj_kernel, heads=heads, q_lora=q_lora, kv_lora=kv_lora,
                          scale=(nope + rope) ** -0.5),
        grid=(b, l // tm),
        in_specs=[row(d), vec, vec, full(w_in_r), full(qn), full(kvn), full(wq), full(wk), full(wv), tab, tab, tab],
        out_specs=[row(heads * 2 * LANES), row(heads * 2 * LANES), row(heads * dv), row(d)],
        out_shape=[jax.ShapeDtypeStruct((b, l, heads * 2 * LANES), BF16),
                   jax.ShapeDtypeStruct((b, l, heads * 2 * LANES), BF16),
                   jax.ShapeDtypeStruct((b, l, heads * dv), BF16),
                   jax.ShapeDtypeStruct((b, l, d), F32)],
        compiler_params=_params("arbitrary", "arbitrary"),
        name="mla_proj",
    )(x, scale, shift, w_in_r, qn, kvn, wq, wk, wv, cq, s1, s2)


def _flash_kernel(q_ref, k_ref, v_ref, o_ref, m_ref, l_ref, acc_ref):
    j = pl.program_id(3)

    @pl.when(j == 0)
    def _():
        m_ref[...] = jnp.full(m_ref.shape, -jnp.inf, F32)
        l_ref[...] = jnp.zeros(l_ref.shape, F32)
        acc_ref[...] = jnp.zeros(acc_ref.shape, F32)

    s = lax.dot_general(q_ref[0], k_ref[0], (((1,), (1,)), ((), ())), preferred_element_type=F32)
    m_old = m_ref[...]
    m_new = jnp.maximum(m_old, jnp.max(s, -1, keepdims=True))
    alpha = jnp.exp(m_old - m_new)
    p = jnp.exp(s - m_new)
    l_ref[...] = alpha * l_ref[...] + jnp.sum(p, -1, keepdims=True)
    acc_ref[...] = alpha * acc_ref[...] + jnp.dot(p.astype(BF16), v_ref[0], preferred_element_type=F32)
    m_ref[...] = m_new

    @pl.when(j == pl.num_programs(3) - 1)
    def _():
        o_ref[0] = acc_ref[...] / l_ref[...]


def _flash(q, k, v, *, heads):
    b, l, _ = q.shape
    lk = k.shape[1]
    tq = _row_tile(l, 1024)
    tk = LANES
    for t in range(LANES, 1280 + 1, LANES):
        if lk % t == 0:
            tk = t
    return pl.pallas_call(
        _flash_kernel,
        grid=(b, heads, l // tq, lk // tk),
        in_specs=[pl.BlockSpec((1, tq, 2 * LANES), lambda bi, h, i, j: (bi, i, h)),
                  pl.BlockSpec((1, tk, 2 * LANES), lambda bi, h, i, j: (bi, j, h)),
                  pl.BlockSpec((1, tk, LANES), lambda bi, h, i, j: (bi, j, h))],
        out_specs=pl.BlockSpec((1, tq, LANES), lambda bi, h, i, j: (bi, i, h)),
        out_shape=jax.ShapeDtypeStruct((b, l, heads * LANES), F32),
        scratch_shapes=[pltpu.VMEM((tq, 1), F32), pltpu.VMEM((tq, 1), F32), pltpu.VMEM((tq, LANES), F32)],
        compiler_params=_params("arbitrary", "arbitrary", "arbitrary", "arbitrary"),
        name="flash",
    )(q, k, v)


def _pad_cols(w):
    n = w.shape[1]
    return jnp.pad(w, ((0, 0), (0, -n % LANES))).astype(BF16)


def _gdn_layer(x, xc, mod, mod_c, w_in, conv_w, a_log, dt_bias, norm_g, w_out, ln_g, ln_b, alpha, need_ctx):
    heads = a_log.shape[1]
    dk = dv = norm_g.shape[0]
    assert dk == LANES and conv_w.shape[1] == 3 * heads * dk
    b = x.shape[0]
    w = _pad_cols(w_in)
    pad = conv_w.shape[0] // 2
    zero = jnp.zeros((b, heads, dk, dv), F32)

    def run(xs, m, s_f, s_b):
        p = _proj(xs, m[1], m[0], w)
        tb = _row_tile(xs.shape[1], SCAN_TILE)
        hp, hn = _conv_halos(p, 3 * heads * dk, tb, pad)
        core = functools.partial(_gdn_core, p, hp, hn, conv_w, a_log, dt_bias, heads=heads)
        o_f, s_f = core(s_f, reverse=False, direction=0)
        o_b, s_b = core(s_b, reverse=True, direction=1)
        return p, o_f, o_b, s_f, s_b

    def finish(xs, m, p, o_f, o_b):
        specs = lambda tm: [_cols(tm, heads * dv, 0), _cols(tm, heads * dv, 0), _cols(tm, heads * dv, 3)]
        return _out("rms", [o_f, o_b, p], specs, [norm_g.reshape(1, dv)], xs, m[2], w_out.astype(BF16),
                    ln_g, ln_b, alpha, width=dv)

    pc, oc_f, oc_b, s_f, s_b = run(xc, mod_c, zero, zero)
    p, o_f, o_b, _, _ = run(x, mod, s_f, s_b)
    x_new = finish(x, mod, p, o_f, o_b)
    xc_new = finish(xc, mod_c, pc, oc_f, oc_b) if need_ctx else None
    return x_new, xc_new


def _gla_layer(x, xc, mod, mod_c, w_in, w_g2, b_g, norm_g, w_out, ln_g, ln_b, alpha, need_ctx):
    dv = norm_g.shape[0]
    rank, qk = w_g2.shape[1], w_g2.shape[2]
    dk = LANES
    heads = qk // dk
    assert dv == 2 * LANES and w_in.shape[1] == 2 * qk + 2 * heads * dv + 2 * rank
    b = x.shape[0]
    w = _pad_cols(w_in)
    zero = jnp.zeros((b, heads, dv, dk), F32)

    def gate_w(direction):
        wg = jnp.zeros((LANES, qk), F32).at[direction * rank:(direction + 1) * rank].set(w_g2[direction])
        return wg.reshape(LANES, heads, dk).transpose(1, 0, 2), b_g[direction].reshape(heads, 1, dk)

    def run(xs, m, s_f, s_b):
        p = _proj(xs, m[1], m[0], w)
        o_f, s_f = _gla_core(p, *gate_w(0), s_f, reverse=False, heads=heads)
        o_b, s_b = _gla_core(p, *gate_w(1), s_b, reverse=True, heads=heads)
        return p, o_f, o_b, s_f, s_b

    def finish(xs, m, p, o_f, o_b):
        specs = lambda tm: [_cols(tm, heads * dv, 0), _cols(tm, heads * dv, 0), _cols(tm, heads * dv, 2)]
        return _out("rms", [o_f, o_b, p], specs, [norm_g.reshape(1, dv)], xs, m[2], w_out.astype(BF16),
                    ln_g, ln_b, alpha, width=dv)

    pc, oc_f, oc_b, s_f, s_b = run(xc, mod_c, zero, zero)
    p, o_f, o_b, _, _ = run(x, mod, s_f, s_b)
    x_new = finish(x, mod, p, o_f, o_b)
    xc_new = finish(xc, mod_c, pc, oc_f, oc_b) if need_ctx else None
    return x_new, xc_new


def _rwkv_layer(x, xc, mod, mod_c, mu, w_rkvz, w0, w1, w2, a0, a1, a2, k_k, k_a, r_k, gn_g, gn_b, w_out,
                ln_g, ln_b, alpha, need_ctx):
    b, _, d = x.shape
    heads, n = r_k.shape
    zero = jnp.zeros((b, heads, n, n), F32)
    r_k2 = jnp.broadcast_to(r_k[None], (2, heads, n))

    def run(xs, m, s_f, s_b):
        r, k, v, z, lw, a = _rwkv_proj(xs, m[1], m[0], mu, w_rkvz, w0, w1, w2, a0, a1, a2)
        core = functools.partial(_rwkv_core, r, k, v, lw, a, k_k, k_a, r_k2)
        y_f, bv_f, s_f = core(s_f, reverse=False, direction=0)
        y_b, bv_b, s_b = core(s_b, reverse=True, direction=1)
        return (y_f, y_b, bv_f, bv_b, z), s_f, s_b

    def finish(xs, m, acts):
        specs = lambda tm: [_cols(tm, d, 0)] * 5
        return _out("rwkv", list(acts), specs, [gn_g.reshape(1, d), gn_b.reshape(1, d)], xs, m[2],
                    w_out.astype(BF16), ln_g, ln_b, alpha, width=n)

    acts_c, s_f, s_b = run(xc, mod_c, zero, zero)
    acts, _, _ = run(x, mod, s_f, s_b)
    x_new = finish(x, mod, acts)
    xc_new = finish(xc, mod_c, acts_c) if need_ctx else None
    return x_new, xc_new


def _rope_tables(n_tokens, rope):
    rows = n_tokens // GRID_W
    row = jnp.repeat(jnp.arange(rows, dtype=F32), GRID_W)
    col = jnp.tile(jnp.arange(GRID_W, dtype=F32), rows)
    n_freq = rope // 4
    inv_freq = ROPE_BASE ** (-jnp.arange(n_freq, dtype=F32) / n_freq)
    ang = jnp.concatenate([row[:, None] * inv_freq, col[:, None] * inv_freq], axis=-1)
    return jnp.cos(ang), jnp.sin(ang)


def _mla_layer(x, xc, mod, mod_c, w_in, q_norm, kv_norm, w_uq, w_ukv, w_out, ln_g, ln_b, alpha):
    b, l, d = x.shape
    lc = xc.shape[1]
    q_lora, kv_lora = q_norm.shape[0], kv_norm.shape[0]
    dv = LANES
    heads = w_out.shape[0] // dv
    rope = w_in.shape[1] - q_lora - kv_lora - heads * dv
    nope = w_uq.shape[1] // heads - rope
    cos, sin = _rope_tables(l, rope)
    proj = functools.partial(_mla_proj, w_in=w_in, q_norm=q_norm, kv_norm=kv_norm, w_uq=w_uq, w_ukv=w_ukv,
                             heads=heads, nope=nope, rope=rope, dv=dv)
    q, k, v, z = proj(x, mod[1], mod[0], cos=cos, sin=sin)
    ones, zeros = jnp.ones((lc, rope // 2), F32), jnp.zeros((lc, rope // 2), F32)
    _, kc, vc, _ = proj(xc, mod_c[1], mod_c[0], cos=ones, sin=zeros)
    o = _flash(q, jnp.concatenate([k, kc], axis=1), jnp.concatenate([v, vc], axis=1), heads=heads)
    specs = lambda tm: [_cols(tm, d, 0), _cols(tm, d, 0)]
    return _out("mla", [o, z], specs, [], x, mod[2], w_out.astype(BF16), ln_g, ln_b, alpha)


def kernel(x, c, ctx, c_ctx, ada_w, ada_b, ln_g, ln_b, gdn_w_in, gdn_conv, gdn_a_log, gdn_dt_bias, gdn_norm, gdn_w_out, rwkv_mu, rwkv_w_rkvz, rwkv_w0, rwkv_w1, rwkv_w2, rwkv_a0, rwkv_a1, rwkv_a2, rwkv_k_k, rwkv_k_a, rwkv_r_k, rwkv_gn_g, rwkv_gn_b, rwkv_w_out, gla_w_in, gla_w_g2, gla_b_g, gla_norm, gla_w_out, mla_w_in, mla_q_norm, mla_kv_norm, mla_w_uq, mla_w_ukv, mla_w_out):
    b, _, d = x.shape
    depth = ada_w.shape[0]
    n_mixers = 4
    assert depth == n_mixers, "one layer of each mixer; the last (MLA) layer needs no context output"
    alpha = (2.0 * depth) ** 0.25

    rows = -(-(b + 1) // SUBLANES) * SUBLANES
    cvec = jnp.concatenate([c, c_ctx[None], jnp.zeros((rows - b - 1, d), F32)], axis=0)
    mods = _ada(cvec, ada_w, ada_b)

    def split(i):
        lat = [mods[i, :b, j * d:(j + 1) * d].reshape(b, 1, d) for j in range(3)]
        con = [jnp.broadcast_to(mods[i, b, j * d:(j + 1) * d].reshape(1, 1, d), (b, 1, d)) for j in range(3)]
        return lat, con

    xc = ctx
    m, mc = split(0)
    x, xc = _gdn_layer(x, xc, m, mc, gdn_w_in[0], gdn_conv[0], gdn_a_log[0], gdn_dt_bias[0], gdn_norm[0],
                       gdn_w_out[0], ln_g[0], ln_b[0], alpha, True)
    m, mc = split(1)
    x, xc = _rwkv_layer(x, xc, m, mc, rwkv_mu[0], rwkv_w_rkvz[0], rwkv_w0[0], rwkv_w1[0], rwkv_w2[0], rwkv_a0[0],
                        rwkv_a1[0], rwkv_a2[0], rwkv_k_k[0], rwkv_k_a[0], rwkv_r_k[0], rwkv_gn_g[0], rwkv_gn_b[0],
                        rwkv_w_out[0], ln_g[1], ln_b[1], alpha, True)
    m, mc = split(2)
    x, xc = _gla_layer(x, xc, m, mc, gla_w_in[0], gla_w_g2[0], gla_b_g[0], gla_norm[0], gla_w_out[0],
                       ln_g[2], ln_b[2], alpha, True)
    m, mc = split(3)
    return _mla_layer(x, xc, m, mc, mla_w_in[0], mla_q_norm[0], mla_kv_norm[0], mla_w_uq[0], mla_w_ukv[0],
                      mla_w_out[0], ln_g[3], ln_b[3], alpha)
```

```python
import functools
import math

import jax
import jax.numpy as jnp
from jax import lax
from jax.experimental import pallas as pl
from jax.experimental.pallas import tpu as pltpu

F32 = jnp.float32
BF16 = jnp.bfloat16
HI = lax.Precision.HIGHEST

LANES = 128
SUBLANES = 8
VMEM_LIMIT = 56 * 1024 * 1024

CHUNK = 64
SUB = 16
LN_EPS = 1e-5
RMS_EPS = 1e-6
GN_EPS = 64e-5
GLA_TAU = 16.0
ROPE_BASE = 10000.0
GRID_W = 64
ROW_TILE = 512
SCAN_TILE = 512
FLASH_ROWS = 256
GDN_HEADS_PER_STEP = 2


def _dot_hi(a, b):
    return jnp.dot(a, b, preferred_element_type=F32, precision=HI)


def _dot(a, b):
    return jnp.dot(a.astype(BF16), b.astype(BF16), preferred_element_type=F32)


def _bdot(a, b):
    return lax.dot_general(a.astype(BF16), b.astype(BF16), (((2,), (1,)), ((0,), (0,))),
                           preferred_element_type=F32)


def _bdot_tn(a, b):
    return lax.dot_general(a.astype(BF16), b.astype(BF16), (((1,), (1,)), ((0,), (0,))),
                           preferred_element_type=F32)


def _bdot_nt(a, b):
    return lax.dot_general(a.astype(BF16), b.astype(BF16), (((2,), (2,)), ((0,), (0,))),
                           preferred_element_type=F32)


def _dot_nt(a, b):
    return lax.dot_general(a.astype(BF16), b.astype(BF16), (((1,), (1,)), ((), ())),
                           preferred_element_type=F32)


def _silu(x):
    return x * jax.nn.sigmoid(x)


def _softplus(x):
    return jnp.maximum(x, 0.0) + jnp.log(1.0 + jnp.exp(-jnp.abs(x)))


def _params(*sem):
    return pltpu.CompilerParams(dimension_semantics=sem, vmem_limit_bytes=VMEM_LIMIT)


def _row_tile(n, cap):
    t = min(n, cap)
    assert n % t == 0 and t % SUBLANES == 0, (n, t)
    return t


def _col_tile(n, cap=768):
    assert n % LANES == 0, n
    best = LANES
    for t in range(LANES, cap + 1, LANES):
        if n % t == 0:
            best = t
    return best


def _masks(reverse, nc):
    r = lax.broadcasted_iota(jnp.int32, (nc, CHUNK, CHUNK), 1)
    c = lax.broadcasted_iota(jnp.int32, (nc, CHUNK, CHUNK), 2)
    if reverse:
        incl, strict = c >= r, c > r
    else:
        incl, strict = c <= r, c < r
    bd = (r // SUB) == (c // SUB)
    eye = (r == c).astype(F32)
    return incl, strict, bd, eye


def _tri_inverse(lm, bd, eye):
    d = jnp.where(bd, lm, 0.0)
    e = lm - d
    d2 = _bdot(d, d)
    d4 = _bdot(d2, d2)
    d8 = _bdot(d4, d4)
    t16 = _bdot(_bdot(eye - d, eye + d2), _bdot(eye + d4, eye + d8))
    f = _bdot(t16, e)
    f2 = _bdot(f, f)
    return _bdot(_bdot(eye - f, eye + f2), t16)


def _chunks(x):
    return x.reshape(x.shape[0] // CHUNK, CHUNK, x.shape[1])


def _cumsum_chunks(x, reverse):
    rows = x.shape[0]
    pos = lax.broadcasted_iota(jnp.int32, (rows, 1), 0) & (CHUNK - 1)
    k = 1
    while k < CHUNK:
        if reverse:
            x = x + jnp.where(pos < CHUNK - k, pltpu.roll(x, rows - k, 0), 0.0)
        else:
            x = x + jnp.where(pos >= k, pltpu.roll(x, k, 0), 0.0)
        k *= 2
    return x


def _chunk_order(nc, reverse):
    return range(nc - 1, -1, -1) if reverse else range(nc)


def _modulate(x, scale, shift):
    return x * (1.0 + scale) + shift


def _ada_kernel(c_ref, w_ref, b_ref, o_ref):
    o_ref[0] = _dot_hi(_silu(c_ref[...]), w_ref[0]) + b_ref[0]


def _ada(cvec, ada_w, ada_b):
    depth, d, n = ada_w.shape
    tn = _col_tile(n, 512)
    rows = cvec.shape[0]
    return pl.pallas_call(
        _ada_kernel,
        grid=(depth, n // tn),
        in_specs=[pl.BlockSpec((rows, d), lambda i, j: (0, 0)),
                  pl.BlockSpec((1, d, tn), lambda i, j: (i, 0, j)),
                  pl.BlockSpec((1, 1, tn), lambda i, j: (i, 0, j))],
        out_specs=pl.BlockSpec((1, rows, tn), lambda i, j: (i, 0, j)),
        out_shape=jax.ShapeDtypeStruct((depth, rows, n), F32),
        compiler_params=_params("arbitrary", "arbitrary"),
        name="ada",
    )(cvec, ada_w, ada_b.reshape(depth, 1, n))


def _proj_kernel(x_ref, sc_ref, sh_ref, w_ref, o_ref, *, tn):
    h = _modulate(x_ref[0], sc_ref[0], sh_ref[0]).astype(BF16)
    for j in range(w_ref.shape[1] // tn):
        o_ref[0, :, j * tn:(j + 1) * tn] = jnp.dot(h, w_ref[:, j * tn:(j + 1) * tn],
                                                   preferred_element_type=F32)


def _proj(x, scale, shift, w):
    b, l, d = x.shape
    n = w.shape[1]
    tm = _row_tile(l, ROW_TILE)
    vec = pl.BlockSpec((1, 1, d), lambda bi, i: (bi, 0, 0))
    return pl.pallas_call(
        functools.partial(_proj_kernel, tn=_col_tile(n)),
        grid=(b, l // tm),
        in_specs=[pl.BlockSpec((1, tm, d), lambda bi, i: (bi, i, 0)), vec, vec,
                  pl.BlockSpec((d, n), lambda bi, i: (0, 0))],
        out_specs=pl.BlockSpec((1, tm, n), lambda bi, i: (bi, i, 0)),
        out_shape=jax.ShapeDtypeStruct((b, l, n), F32),
        compiler_params=_params("arbitrary", "arbitrary"),
        name="proj",
    )(x, scale, shift, w)


def _group_rms(o, g, width):
    parts = []
    for j in range(o.shape[1] // width):
        oj = o[:, j * width:(j + 1) * width]
        parts.append(oj * lax.rsqrt(jnp.mean(oj * oj, -1, keepdims=True) + RMS_EPS) * g)
    return jnp.concatenate(parts, axis=1)


def _out_tail(pre, x_ref, gate_ref, w_ref, lg_ref, lb_ref, o_ref, alpha):
    y = jnp.dot(pre.astype(BF16), w_ref[...], preferred_element_type=F32)
    r = alpha * x_ref[0] + gate_ref[0] * y
    rc = r - jnp.mean(r, -1, keepdims=True)
    o_ref[0] = rc * lax.rsqrt(jnp.mean(rc * rc, -1, keepdims=True) + LN_EPS) * lg_ref[...] + lb_ref[...]


def _out_rms_kernel(of_ref, ob_ref, z_ref, g_ref, x_ref, gate_ref, w_ref, lg_ref, lb_ref, o_ref, *, alpha, width):
    o = _group_rms(of_ref[0] + ob_ref[0], g_ref[...], width)
    _out_tail(o * _silu(z_ref[0]), x_ref, gate_ref, w_ref, lg_ref, lb_ref, o_ref, alpha)


def _out_rwkv_kernel(yf_ref, yb_ref, bf_ref, bb_ref, z_ref, g_ref, gb_ref, x_ref, gate_ref, w_ref, lg_ref, lb_ref,
                     o_ref, *, alpha, width):
    y = yf_ref[0] + yb_ref[0]
    parts = []
    for j in range(y.shape[1] // width):
        yj = y[:, j * width:(j + 1) * width]
        yc = yj - jnp.mean(yj, -1, keepdims=True)
        parts.append(yc * lax.rsqrt(jnp.mean(yc * yc, -1, keepdims=True) + GN_EPS))
    yn = jnp.concatenate(parts, axis=1) * g_ref[...] + gb_ref[...]
    pre = (yn + bf_ref[0] + bb_ref[0]) * _silu(z_ref[0])
    _out_tail(pre, x_ref, gate_ref, w_ref, lg_ref, lb_ref, o_ref, alpha)


def _out_mla_kernel(o_in_ref, z_ref, x_ref, gate_ref, w_ref, lg_ref, lb_ref, o_ref, *, alpha):
    _out_tail(o_in_ref[0] * _silu(z_ref[0]), x_ref, gate_ref, w_ref, lg_ref, lb_ref, o_ref, alpha)


def _out(kind, acts, act_specs, vecs, x, gate, w_out, ln_g, ln_b, alpha, width=None):
    b, l, d = x.shape
    tm = _row_tile(l, ROW_TILE)
    k = w_out.shape[0]
    body = {"rms": functools.partial(_out_rms_kernel, alpha=alpha, width=width),
            "rwkv": functools.partial(_out_rwkv_kernel, alpha=alpha, width=width),
            "mla": functools.partial(_out_mla_kernel, alpha=alpha)}[kind]
    row = lambda n: pl.BlockSpec((1, n), lambda bi, i: (0, 0))
    in_specs = (list(act_specs(tm)) + [row(v.shape[1]) for v in vecs]
                + [pl.BlockSpec((1, tm, d), lambda bi, i: (bi, i, 0)),
                   pl.BlockSpec((1, 1, d), lambda bi, i: (bi, 0, 0)),
                   pl.BlockSpec((k, d), lambda bi, i: (0, 0)), row(d), row(d)])
    return pl.pallas_call(
        body,
        grid=(b, l // tm),
        in_specs=in_specs,
        out_specs=pl.BlockSpec((1, tm, d), lambda bi, i: (bi, i, 0)),
        out_shape=jax.ShapeDtypeStruct((b, l, d), F32),
        compiler_params=_params("arbitrary", "arbitrary"),
        name="out_" + kind,
    )(*acts, *vecs, x, gate, w_out, ln_g.reshape(1, d), ln_b.reshape(1, d))


def _cols(tm, width, col):
    return pl.BlockSpec((1, tm, width), lambda bi, i: (bi, i, col))


def _gdn_kernel(q_ref, k_ref, v_ref, hp_q, hp_k, hp_v, hn_q, hn_k, hn_v, cw_q, cw_k, cw_v, ab_ref, al_ref, dt_ref,
                s0_ref, o_ref, sf_ref, ext_ref, s_ref, *, reverse, direction, heads, taps):
    hg = pl.program_id(1)
    i = pl.program_id(2)
    tb = q_ref.shape[1]
    nc = tb // CHUNK
    pad = taps // 2
    per = s_ref.shape[0]

    @pl.when(i == 0)
    def _():
        s_ref[...] = s0_ref[0]

    def conv(x_ref, hp_ref, hn_ref, cw_ref, slot):
        ext_ref[slot, 0:SUBLANES, :] = hp_ref[0, 0]
        ext_ref[slot, SUBLANES:SUBLANES + tb, :] = x_ref[0]
        ext_ref[slot, SUBLANES + tb:, :] = hn_ref[0, 0]
        acc = jnp.zeros((tb, per * LANES), F32)
        for t in range(taps):
            acc = acc + ext_ref[slot, pl.ds(SUBLANES - pad + t, tb), :] * cw_ref[t:t + 1, :]
        return _silu(acc)

    def stack(x):
        return jnp.concatenate([_chunks(x[:, hh * LANES:(hh + 1) * LANES]) for hh in range(per)], axis=0)

    q = stack(conv(q_ref, hp_q, hn_q, cw_q, 0))
    k = stack(conv(k_ref, hp_k, hn_k, cw_k, 1))
    v = stack(conv(v_ref, hp_v, hn_v, cw_v, 2))
    q = q * lax.rsqrt(jnp.sum(q * q, -1, keepdims=True) + RMS_EPS) * (LANES ** -0.5)
    k = k * lax.rsqrt(jnp.sum(k * k, -1, keepdims=True) + RMS_EPS)

    ab = ab_ref[0]
    lane = lax.broadcasted_iota(jnp.int32, (1, LANES), 1)
    betas, gs, gcs = [], [], []
    for hh in range(per):
        col = direction * 2 * heads + hg * per + hh
        betas.append(jax.nn.sigmoid(jnp.sum(jnp.where(lane == col, ab, 0.0), -1, keepdims=True)))
        gpre = jnp.sum(jnp.where(lane == col + heads, ab, 0.0), -1, keepdims=True)
        g_h = -jnp.exp(al_ref[hh]) * _softplus(gpre + dt_ref[hh])
        gs.append(_chunks(g_h))
        gcs.append(_chunks(_cumsum_chunks(g_h, reverse)))
    beta = jnp.concatenate(betas, axis=0).reshape(per * nc, CHUNK, 1)
    g_b = jnp.concatenate(gs, axis=0)
    gc_b = jnp.concatenate(gcs, axis=0)

    incl, strict, bd, eye = _masks(reverse, per * nc)
    gtot = jnp.sum(g_b, axis=1, keepdims=True)
    m = gc_b[:, :, :CHUNK] - jnp.swapaxes(gc_b, 1, 2)[:, :CHUNK, :]
    decay = jnp.exp(jnp.where(incl, m, -jnp.inf))
    kb = k * beta
    lm = jnp.where(strict, _bdot_nt(kb, k) * decay, 0.0)
    t = _tri_inverse(lm, bd, eye)
    eg = jnp.exp(gc_b)
    u0 = _bdot(t, v * beta)
    wk = _bdot(t, kb * eg)
    qk = _bdot_nt(q, k) * decay
    kd = k * jnp.exp(gtot - gc_b)
    gl = jnp.exp(gtot)
    ms = _bdot_tn(kd, wk)
    c0 = _bdot_tn(kd, u0)
    qq = q * eg - _bdot(qk, wk)
    oc = _bdot(qk, u0)

    s = [s_ref[hh] for hh in range(per)]
    for c in _chunk_order(nc, reverse):
        for hh in range(per):
            j = hh * nc + c
            o_ref[0, c * CHUNK:(c + 1) * CHUNK, hh * LANES:(hh + 1) * LANES] = _dot(qq[j], s[hh]) + oc[j]
            s[hh] = s[hh] * gl[j] - _dot(ms[j], s[hh]) + c0[j]
    for hh in range(per):
        s_ref[hh] = s[hh]
        sf_ref[0, hh] = s[hh]


def _gdn_core(p, halo_prev, halo_next, conv_w, a_log, dt_bias, s0, *, reverse, direction, heads):
    b, l, _ = p.shape
    tb = _row_tile(l, SCAN_TILE)
    nblk = l // tb
    per = GDN_HEADS_PER_STEP
    assert heads % per == 0
    width = per * LANES
    taps = conv_w.shape[0]
    conv_w = jnp.pad(conv_w, ((0, SUBLANES - taps), (0, 0)))
    blk = (lambda i: nblk - 1 - i) if reverse else (lambda i: i)
    al = jnp.broadcast_to(a_log[direction][:, None, None], (heads, 1, LANES))
    dt = jnp.broadcast_to(dt_bias[direction][:, None, None], (heads, 1, LANES))

    def act(off):
        return pl.BlockSpec((1, tb, width), lambda bi, h, i: (bi, blk(i), off // per + h))

    def halo(off):
        return pl.BlockSpec((1, 1, SUBLANES, width), lambda bi, h, i: (bi, blk(i), 0, off // per + h))

    def cw(off):
        return pl.BlockSpec((SUBLANES, width), lambda bi, h, i: (0, off // per + h))

    per_head = pl.BlockSpec((per, 1, LANES), lambda bi, h, i: (h, 0, 0))
    state = pl.BlockSpec((1, per, LANES, LANES), lambda bi, h, i: (bi, h, 0, 0))
    ab_col = (3 * heads * LANES + heads * LANES) // LANES
    return pl.pallas_call(
        functools.partial(_gdn_kernel, reverse=reverse, direction=direction, heads=heads, taps=taps),
        grid=(b, heads // per, nblk),
        in_specs=[act(0), act(heads), act(2 * heads),
                  halo(0), halo(heads), halo(2 * heads), halo(0), halo(heads), halo(2 * heads),
                  cw(0), cw(heads), cw(2 * heads),
                  pl.BlockSpec((1, tb, LANES), lambda bi, h, i: (bi, blk(i), ab_col)),
                  per_head, per_head, state],
        out_specs=[pl.BlockSpec((1, tb, width), lambda bi, h, i: (bi, blk(i), h)), state],
        out_shape=[jax.ShapeDtypeStruct((b, l, heads * LANES), F32),
                   jax.ShapeDtypeStruct(s0.shape, F32)],
        scratch_shapes=[pltpu.VMEM((3, tb + 2 * SUBLANES, width), F32), pltpu.VMEM((per, LANES, LANES), F32)],
        compiler_params=_params("arbitrary", "arbitrary", "arbitrary"),
        name="gdn_bwd" if reverse else "gdn_fwd",
    )(p, p, p, halo_prev, halo_prev, halo_prev, halo_next, halo_next, halo_next, conv_w, conv_w, conv_w, p, al, dt, s0)


def _conv_halos(p, width, tb, pad):
    b, l, _ = p.shape
    nblk = l // tb
    p5 = p[:, :, :width].reshape(b, nblk, tb, width)
    zeros = jnp.zeros((b, 1, pad, width), p.dtype)
    prev = jnp.concatenate([zeros, p5[:, :-1, tb - pad:, :]], axis=1)
    nxt = jnp.concatenate([p5[:, 1:, :pad, :], zeros], axis=1)
    prev = jnp.pad(prev, ((0, 0), (0, 0), (SUBLANES - pad, 0), (0, 0)))
    nxt = jnp.pad(nxt, ((0, 0), (0, 0), (0, SUBLANES - pad), (0, 0)))
    return prev, nxt


def _gla_kernel(q_ref, k_ref, v_ref, gr_ref, wg_ref, bg_ref, s0_ref, o_ref, sf_ref, s_ref, *, reverse):
    i = pl.program_id(2)
    tb = q_ref.shape[1]
    nc = tb // CHUNK

    @pl.when(i == 0)
    def _():
        s_ref[...] = s0_ref[0, 0]

    q = _chunks(q_ref[0] * (LANES ** -0.5))
    k = _chunks(k_ref[0])
    v = _chunks(v_ref[0])
    gpre = _dot_hi(gr_ref[0], wg_ref[0]) + bg_ref[0]
    g = -_softplus(-gpre) / GLA_TAU

    incl, _, _, _ = _masks(reverse, nc)
    bc = _chunks(_cumsum_chunks(g, reverse))
    btot = jnp.sum(_chunks(g), axis=1, keepdims=True)
    mid = CHUNK - 1 - CHUNK // 2 if reverse else CHUNK // 2
    ref = bc[:, mid:mid + 1, :]
    att = jnp.where(incl, _bdot_nt(q * jnp.exp(bc - ref), k * jnp.exp(ref - bc)), 0.0)
    o_intra = _bdot(att, v)
    qg = q * jnp.exp(bc)
    kv = _bdot_tn(v, k * jnp.exp(btot - bc))
    gl = jnp.exp(btot)

    st = s_ref[...]
    for c in _chunk_order(nc, reverse):
        o_ref[0, c * CHUNK:(c + 1) * CHUNK, :] = o_intra[c] + _dot_nt(qg[c], st)
        st = st * gl[c] + kv[c]
    s_ref[...] = st
    sf_ref[0, 0] = st


def _gla_core(p, wg, bg, s0, *, reverse, heads):
    b, l, _ = p.shape
    tb = _row_tile(l, SCAN_TILE)
    nblk = l // tb
    dk, dv = LANES, 2 * LANES
    blk = (lambda i: nblk - 1 - i) if reverse else (lambda i: i)
    gr_col = (2 * heads * dk + 2 * heads * dv) // LANES
    state = pl.BlockSpec((1, 1, dv, dk), lambda bi, h, i: (bi, h, 0, 0))
    return pl.pallas_call(
        functools.partial(_gla_kernel, reverse=reverse),
        grid=(b, heads, nblk),
        in_specs=[pl.BlockSpec((1, tb, dk), lambda bi, h, i: (bi, blk(i), h)),
                  pl.BlockSpec((1, tb, dk), lambda bi, h, i: (bi, blk(i), heads + h)),
                  pl.BlockSpec((1, tb, dv), lambda bi, h, i: (bi, blk(i), heads + h)),
                  pl.BlockSpec((1, tb, LANES), lambda bi, h, i: (bi, blk(i), gr_col)),
                  pl.BlockSpec((1, LANES, dk), lambda bi, h, i: (h, 0, 0)),
                  pl.BlockSpec((1, 1, dk), lambda bi, h, i: (h, 0, 0)),
                  state],
        out_specs=[pl.BlockSpec((1, tb, dv), lambda bi, h, i: (bi, blk(i), h)), state],
        out_shape=[jax.ShapeDtypeStruct((b, l, heads * dv), F32), jax.ShapeDtypeStruct(s0.shape, F32)],
        scratch_shapes=[pltpu.VMEM((dv, dk), F32)],
        compiler_params=_params("arbitrary", "arbitrary", "arbitrary"),
        name="gla_bwd" if reverse else "gla_fwd",
    )(p, p, p, p, wg, bg, s0)


def _rwkv_proj_kernel(x_ref, xp_ref, xn_ref, sc_ref, sh_ref, mu_ref, w4_ref, w1_ref, a1_ref, w2_ref, a2_ref,
                      w0_ref, a0_ref, r_ref, k_ref, v_ref, z_ref, lw_ref, a_ref):
    i = pl.program_id(1)
    last = pl.num_programs(1) - 1
    tm = x_ref.shape[1]
    rank = w2_ref.shape[1]
    sc, sh = sc_ref[0], sh_ref[0]
    u = _modulate(x_ref[0], sc, sh)
    up = jnp.where(i > 0, _modulate(xp_ref[0, 0, SUBLANES - 1:SUBLANES, :], sc, sh), 0.0)
    un = jnp.where(i < last, _modulate(xn_ref[0, 0, 0:1, :], sc, sh), 0.0)
    row = lax.broadcasted_iota(jnp.int32, (tm, 1), 0)
    prev = jnp.where(row == 0, up, pltpu.roll(u, 1, 0))
    nxt = jnp.where(row == tm - 1, un, pltpu.roll(u, tm - 1, 0))
    xx = 0.5 * (prev + nxt) - u

    def mix(j):
        return (u + xx * mu_ref[j:j + 1, :]).astype(BF16)

    r_ref[0] = jnp.dot(mix(0), w4_ref[0], preferred_element_type=F32)
    k_ref[0] = jnp.dot(mix(2), w4_ref[1], preferred_element_type=F32)
    v_ref[0] = jnp.dot(mix(3), w4_ref[2], preferred_element_type=F32)
    z_ref[0] = jnp.dot(mix(5), w4_ref[3], preferred_element_type=F32)
    lw1 = jnp.tanh(jnp.dot(mix(1), w1_ref[...], preferred_element_type=F32))
    a1 = jnp.dot(mix(4), a1_ref[...], preferred_element_type=F32)
    for n in range(2):
        wlog = w0_ref[n:n + 1, :] + _dot(lw1[:, n * rank:(n + 1) * rank], w2_ref[n])
        lw_ref[n, 0] = -jnp.exp(-_softplus(-wlog) - 0.5)
        a_ref[n, 0] = jax.nn.sigmoid(a0_ref[n:n + 1, :] + _dot(a1[:, n * rank:(n + 1) * rank], a2_ref[n]))


def _rwkv_proj(x, scale, shift, mu, w_rkvz, w0, w1, w2, a0, a1, a2):
    b, l, d = x.shape
    tm = _row_tile(l, 256)
    nblk = l // tm
    rank = w1.shape[2]
    x4 = x.reshape(b, nblk, tm, d)
    zeros = jnp.zeros((b, 1, SUBLANES, d), x.dtype)
    xp = jnp.concatenate([zeros, x4[:, :-1, tm - SUBLANES:, :]], axis=1)
    xn = jnp.concatenate([x4[:, 1:, :SUBLANES, :], zeros], axis=1)
    w1c = jnp.concatenate([w1[0], w1[1]], axis=1).astype(BF16)
    a1c = jnp.concatenate([a1[0], a1[1]], axis=1).astype(BF16)
    vec = pl.BlockSpec((1, 1, d), lambda bi, i: (bi, 0, 0))
    full = lambda a: pl.BlockSpec(a.shape, lambda bi, i: (0,) * a.ndim)
    halo = pl.BlockSpec((1, 1, SUBLANES, d), lambda bi, i: (bi, i, 0, 0))
    act = pl.BlockSpec((1, tm, d), lambda bi, i: (bi, i, 0))
    act2 = pl.BlockSpec((2, 1, tm, d), lambda bi, i: (0, bi, i, 0))
    w4 = w_rkvz.astype(BF16)
    w2b, a2b = w2.astype(BF16), a2.astype(BF16)
    sds = jax.ShapeDtypeStruct((b, l, d), F32)
    sds2 = jax.ShapeDtypeStruct((2, b, l, d), F32)
    return pl.pallas_call(
        _rwkv_proj_kernel,
        grid=(b, nblk),
        in_specs=[act, halo, halo, vec, vec, full(mu), full(w4), full(w1c), full(a1c), full(w2b), full(a2b),
                  full(w0), full(a0)],
        out_specs=[act, act, act, act, act2, act2],
        out_shape=[sds, sds, sds, sds, sds2, sds2],
        compiler_params=_params("arbitrary", "arbitrary"),
        name="rwkv_proj",
    )(x, xp, xn, scale, shift, mu, w4, w1c, a1c, w2b, a2b, w0, a0)


def _rwkv_kernel(r_ref, k_ref, v_ref, lw_ref, a_ref, kk_ref, ka_ref, rk_ref, s0_ref, y_ref, bv_ref, sf_ref, s_ref,
                 *, reverse, n):
    i = pl.program_id(2)
    tb = r_ref.shape[1]
    nc = tb // CHUNK
    per = LANES // n

    @pl.when(i == 0)
    def _():
        s_ref[...] = s0_ref[0]

    def heads(x):
        return jnp.concatenate([_chunks(x[:, hh * n:(hh + 1) * n]) for hh in range(per)], axis=0)

    r, kr, v, lw, a = r_ref[0], k_ref[0], v_ref[0], lw_ref[0, 0], a_ref[0, 0]
    kd = kr * (1.0 + (a - 1.0) * ka_ref[...])
    gc = _cumsum_chunks(lw, reverse)
    kq, kdh, rkd = heads(kr * kk_ref[...]), heads(kd), heads(r * kd * rk_ref[...])
    ah, rh, vh, lwc, gc = heads(a), heads(r), heads(v), heads(lw), heads(gc)
    kk = kq * lax.rsqrt(jnp.sum(kq * kq, -1, keepdims=True) + RMS_EPS)
    kka = kk * ah
    bv = jnp.sum(rkd, -1, keepdims=True) * vh
    gtot = jnp.sum(lwc, axis=1, keepdims=True)
    gcx = gc - lwc
    ref = 0.5 * gtot
    e_in = jnp.exp(ref - gc)
    e_out = jnp.exp(gtot - gc)
    gl = jnp.exp(gtot)

    incl, strict, bd, eye = _masks(reverse, per * nc)
    at = kk * jnp.exp(gcx - ref)
    bt = kka * e_in
    kt = kdh * e_in
    rt = rh * jnp.exp(gc - ref)
    a_ab = jnp.where(strict, _bdot_nt(at, bt), 0.0)
    a_ak = jnp.where(strict, _bdot_nt(at, kt), 0.0)
    a_rb = jnp.where(incl, _bdot_nt(rt, bt), 0.0)
    a_rk = jnp.where(incl, _bdot_nt(rt, kt), 0.0)
    t = _tri_inverse(a_ab, bd, eye)
    u0 = -_bdot(t, _bdot(a_ak, vh))
    w = _bdot(t, kk * jnp.exp(gcx))
    bh = kka * e_out
    ms = _bdot_tn(w, bh)
    c0 = _bdot_tn(u0, bh) + _bdot_tn(vh, kdh * e_out)
    rq = rh * jnp.exp(gc) - _bdot(a_rb, w)
    yc = _bdot(a_rb, u0) + _bdot(a_rk, vh)

    st = [s_ref[hh] for hh in range(per)]
    for c in _chunk_order(nc, reverse):
        for hh in range(per):
            j = hh * nc + c
            y_ref[0, c * CHUNK:(c + 1) * CHUNK, hh * n:(hh + 1) * n] = _dot_nt(rq[j], st[hh]) + yc[j]
            st[hh] = st[hh] * gl[j] - _dot(st[hh], ms[j]) + c0[j]
    for hh in range(per):
        bv_ref[0, :, hh * n:(hh + 1) * n] = bv[hh * nc:(hh + 1) * nc].reshape(tb, n)
        s_ref[hh] = st[hh]
        sf_ref[0, hh] = st[hh]


def _rwkv_core(r, k, v, lw, a, k_k, k_a, r_k, s0, *, reverse, direction):
    b, l, d = r.shape
    n = s0.shape[-1]
    per = LANES // n
    tb = _row_tile(l, SCAN_TILE)
    nblk = l // tb
    blk = (lambda i: nblk - 1 - i) if reverse else (lambda i: i)
    act = pl.BlockSpec((1, tb, LANES), lambda bi, h, i: (bi, blk(i), h))
    act2 = pl.BlockSpec((1, 1, tb, LANES), lambda bi, h, i: (direction, bi, blk(i), h))
    vec = pl.BlockSpec((1, LANES), lambda bi, h, i: (0, h))
    state = pl.BlockSpec((1, per, n, n), lambda bi, h, i: (bi, h, 0, 0))
    sds = jax.ShapeDtypeStruct((b, l, d), F32)
    return pl.pallas_call(
        functools.partial(_rwkv_kernel, reverse=reverse, n=n),
        grid=(b, d // LANES, nblk),
        in_specs=[act, act, act, act2, act2, vec, vec, vec, state],
        out_specs=[act, act, state],
        out_shape=[sds, sds, jax.ShapeDtypeStruct(s0.shape, F32)],
        scratch_shapes=[pltpu.VMEM((per, n, n), F32)],
        compiler_params=_params("arbitrary", "arbitrary", "arbitrary"),
        name="rwkv_bwd" if reverse else "rwkv_fwd",
    )(r, k, v, lw, a, k_k.reshape(1, d), k_a.reshape(1, d), r_k[direction].reshape(1, d), s0)


def _mla_proj_kernel(x_ref, sc_ref, sh_ref, win_ref, qn_ref, kvn_ref, wq_ref, wk_ref, wv_ref, cq_ref, s1_ref, s2_ref,
                     q_ref, k_ref, v_ref, z_ref, *, heads, q_lora, kv_lora, scale):
    d = z_ref.shape[2]
    h = _modulate(x_ref[0], sc_ref[0], sh_ref[0]).astype(BF16)
    p = jnp.dot(h, win_ref[...], preferred_element_type=F32)

    def rms(t, g):
        return t * lax.rsqrt(jnp.mean(t * t, -1, keepdims=True) + RMS_EPS) * g

    def rope(t):
        return (t * cq_ref[...] + pltpu.roll(t, LANES - LANES // 4, 1) * s1_ref[...]
                + pltpu.roll(t, LANES // 4, 1) * s2_ref[...])

    ql = rms(p[:, :q_lora], qn_ref[...]).astype(BF16)
    kvl = rms(p[:, q_lora:q_lora + kv_lora], kvn_ref[...]).astype(BF16)
    z_ref[0] = p[:, q_lora + kv_lora:q_lora + kv_lora + d]
    kr = rope(p[:, q_lora + kv_lora + d:])
    q = jnp.dot(ql, wq_ref[...], preferred_element_type=F32) * scale
    kn = jnp.dot(kvl, wk_ref[...], preferred_element_type=F32)
    v_ref[0] = jnp.dot(kvl, wv_ref[...], preferred_element_type=F32).astype(BF16)
    for j in range(heads):
        base = 2 * LANES * j
        q_ref[0, :, base:base + LANES] = q[:, base:base + LANES].astype(BF16)
        q_ref[0, :, base + LANES:base + 2 * LANES] = rope(q[:, base + LANES:base + 2 * LANES]).astype(BF16)
        k_ref[0, :, base:base + LANES] = kn[:, j * LANES:(j + 1) * LANES].astype(BF16)
        k_ref[0, :, base + LANES:base + 2 * LANES] = kr.astype(BF16)


def _mla_proj(x, scale, shift, w_in, q_norm, kv_norm, w_uq, w_ukv, cos, sin, *, heads, nope, rope, dv):
    b, l, d = x.shape
    q_lora, kv_lora = q_norm.shape[0], kv_norm.shape[0]
    tm = _row_tile(l, 256)
    half = rope // 2
    assert nope == LANES and dv == LANES and rope == LANES // 2
    o2 = q_lora + kv_lora
    w_in_r = jnp.concatenate([w_in[:, :o2], w_in[:, o2 + rope:], w_in[:, o2:o2 + rope],
                              jnp.zeros((d, LANES - rope), w_in.dtype)], axis=1).astype(BF16)
    wq = w_uq.reshape(q_lora, heads, nope + rope)
    wq = jnp.concatenate([wq, jnp.zeros((q_lora, heads, LANES - rope), w_uq.dtype)], axis=2)
    wq = wq.reshape(q_lora, heads * 2 * LANES).astype(BF16)
    wkv = w_ukv.reshape(kv_lora, heads, nope + dv)
    wk = wkv[:, :, :nope].reshape(kv_lora, heads * nope).astype(BF16)
    wv = wkv[:, :, nope:].reshape(kv_lora, heads * dv).astype(BF16)
    zer = jnp.zeros((l, half), F32)
    zer2 = jnp.zeros((l, LANES - rope), F32)
    cq = jnp.concatenate([cos, cos, zer2], axis=1)
    s1 = jnp.concatenate([-sin, zer, zer2], axis=1)
    s2 = jnp.concatenate([zer, sin, zer2], axis=1)
    vec = pl.BlockSpec((1, 1, d), lambda bi, i: (bi, 0, 0))
    full = lambda a: pl.BlockSpec(a.shape, lambda bi, i: (0,) * a.ndim)
    tab = pl.BlockSpec((tm, LANES), lambda bi, i: (i, 0))
    row = lambda w: pl.BlockSpec((1, tm, w), lambda bi, i: (bi, i, 0))
    qn = q_norm.reshape(1, q_lora)
    kvn = kv_norm.reshape(1, kv_lora)
    return pl.pallas_call(
        functools.partial(_mla_proj_kernel, heads=heads, q_lora=q_lora, kv_lora=kv_lora,
                          scale=(nope + rope) ** -0.5 * math.log2(math.e)),
        grid=(b, l // tm),
        in_specs=[row(d), vec, vec, full(w_in_r), full(qn), full(kvn), full(wq), full(wk), full(wv), tab, tab, tab],
        out_specs=[row(heads * 2 * LANES), row(heads * 2 * LANES), row(heads * dv), row(d)],
        out_shape=[jax.ShapeDtypeStruct((b, l, heads * 2 * LANES), BF16),
                   jax.ShapeDtypeStruct((b, l, heads * 2 * LANES), BF16),
                   jax.ShapeDtypeStruct((b, l, heads * dv), BF16),
                   jax.ShapeDtypeStruct((b, l, d), F32)],
        compiler_params=_params("arbitrary", "arbitrary"),
        name="mla_proj",
    )(x, scale, shift, w_in_r, qn, kvn, wq, wk, wv, cq, s1, s2)


def _flash_kernel(q_ref, k_ref, v_ref, o_ref, m_ref, l_ref, acc_ref):
    j = pl.program_id(3)

    @pl.when(j == 0)
    def _():
        m_ref[...] = jnp.full(m_ref.shape, -jnp.inf, F32)
        l_ref[...] = jnp.zeros(l_ref.shape, F32)
        acc_ref[...] = jnp.zeros(acc_ref.shape, F32)

    groups = q_ref.shape[1] // FLASH_ROWS

    def scores(g):
        return lax.dot_general(q_ref[0, g * FLASH_ROWS:(g + 1) * FLASH_ROWS, :], k_ref[0],
                               (((1,), (1,)), ((), ())), preferred_element_type=F32)

    s_next = scores(0)
    for g in range(groups):
        rows = slice(g * FLASH_ROWS, (g + 1) * FLASH_ROWS)
        s = s_next
        if g + 1 < groups:
            s_next = scores(g + 1)
        m_old = m_ref[rows, :]
        m_new = jnp.maximum(m_old, jnp.max(s, -1, keepdims=True))
        alpha = jnp.exp2(m_old - m_new)
        p = jnp.exp2(s - m_new)
        l_ref[rows, :] = alpha * l_ref[rows, :] + jnp.sum(p, -1, keepdims=True)
        acc_ref[rows, :] = alpha * acc_ref[rows, :] + jnp.dot(p.astype(BF16), v_ref[0], preferred_element_type=F32)
        m_ref[rows, :] = m_new

    @pl.when(j == pl.num_programs(3) - 1)
    def _():
        o_ref[0] = acc_ref[...] / l_ref[...]


def _flash(q, k, v, *, heads):
    b, l, _ = q.shape
    lk = k.shape[1]
    tq = _row_tile(l, 1024)
    tk = LANES
    for t in range(LANES, 1280 + 1, LANES):
        if lk % t == 0:
            tk = t
    return pl.pallas_call(
        _flash_kernel,
        grid=(b, heads, l // tq, lk // tk),
        in_specs=[pl.BlockSpec((1, tq, 2 * LANES), lambda bi, h, i, j: (bi, i, h)),
                  pl.BlockSpec((1, tk, 2 * LANES), lambda bi, h, i, j: (bi, j, h)),
                  pl.BlockSpec((1, tk, LANES), lambda bi, h, i, j: (bi, j, h))],
        out_specs=pl.BlockSpec((1, tq, LANES), lambda bi, h, i, j: (bi, i, h)),
        out_shape=jax.ShapeDtypeStruct((b, l, heads * LANES), F32),
        scratch_shapes=[pltpu.VMEM((tq, 1), F32), pltpu.VMEM((tq, 1), F32), pltpu.VMEM((tq, LANES), F32)],
        compiler_params=_params("arbitrary", "arbitrary", "arbitrary", "arbitrary"),
        name="flash",
    )(q, k, v)


def _pad_cols(w):
    n = w.shape[1]
    return jnp.pad(w, ((0, 0), (0, -n % LANES))).astype(BF16)


def _gdn_layer(x, xc, mod, mod_c, w_in, conv_w, a_log, dt_bias, norm_g, w_out, ln_g, ln_b, alpha, need_ctx):
    heads = a_log.shape[1]
    dk = dv = norm_g.shape[0]
    assert dk == LANES and conv_w.shape[1] == 3 * heads * dk
    b = x.shape[0]
    w = _pad_cols(w_in)
    pad = conv_w.shape[0] // 2
    zero = jnp.zeros((b, heads, dk, dv), F32)

    def run(xs, m, s_f, s_b):
        p = _proj(xs, m[1], m[0], w)
        tb = _row_tile(xs.shape[1], SCAN_TILE)
        hp, hn = _conv_halos(p, 3 * heads * dk, tb, pad)
        core = functools.partial(_gdn_core, p, hp, hn, conv_w, a_log, dt_bias, heads=heads)
        o_f, s_f = core(s_f, reverse=False, direction=0)
        o_b, s_b = core(s_b, reverse=True, direction=1)
        return p, o_f, o_b, s_f, s_b

    def finish(xs, m, p, o_f, o_b):
        specs = lambda tm: [_cols(tm, heads * dv, 0), _cols(tm, heads * dv, 0), _cols(tm, heads * dv, 3)]
        return _out("rms", [o_f, o_b, p], specs, [norm_g.reshape(1, dv)], xs, m[2], w_out.astype(BF16),
                    ln_g, ln_b, alpha, width=dv)

    pc, oc_f, oc_b, s_f, s_b = run(xc, mod_c, zero, zero)
    p, o_f, o_b, _, _ = run(x, mod, s_f, s_b)
    x_new = finish(x, mod, p, o_f, o_b)
    xc_new = finish(xc, mod_c, pc, oc_f, oc_b) if need_ctx else None
    return x_new, xc_new


def _gla_layer(x, xc, mod, mod_c, w_in, w_g2, b_g, norm_g, w_out, ln_g, ln_b, alpha, need_ctx):
    dv = norm_g.shape[0]
    rank, qk = w_g2.shape[1], w_g2.shape[2]
    dk = LANES
    heads = qk // dk
    assert dv == 2 * LANES and w_in.shape[1] == 2 * qk + 2 * heads * dv + 2 * rank
    b = x.shape[0]
    w = _pad_cols(w_in)
    zero = jnp.zeros((b, heads, dv, dk), F32)

    def gate_w(direction):
        wg = jnp.zeros((LANES, qk), F32).at[direction * rank:(direction + 1) * rank].set(w_g2[direction])
        return wg.reshape(LANES, heads, dk).transpose(1, 0, 2), b_g[direction].reshape(heads, 1, dk)

    def run(xs, m, s_f, s_b):
        p = _proj(xs, m[1], m[0], w)
        o_f, s_f = _gla_core(p, *gate_w(0), s_f, reverse=False, heads=heads)
        o_b, s_b = _gla_core(p, *gate_w(1), s_b, reverse=True, heads=heads)
        return p, o_f, o_b, s_f, s_b

    def finish(xs, m, p, o_f, o_b):
        specs = lambda tm: [_cols(tm, heads * dv, 0), _cols(tm, heads * dv, 0), _cols(tm, heads * dv, 2)]
        return _out("rms", [o_f, o_b, p], specs, [norm_g.reshape(1, dv)], xs, m[2], w_out.astype(BF16),
                    ln_g, ln_b, alpha, width=dv)

    pc, oc_f, oc_b, s_f, s_b = run(xc, mod_c, zero, zero)
    p, o_f, o_b, _, _ = run(x, mod, s_f, s_b)
    x_new = finish(x, mod, p, o_f, o_b)
    xc_new = finish(xc, mod_c, pc, oc_f, oc_b) if need_ctx else None
    return x_new, xc_new


def _rwkv_layer(x, xc, mod, mod_c, mu, w_rkvz, w0, w1, w2, a0, a1, a2, k_k, k_a, r_k, gn_g, gn_b, w_out,
                ln_g, ln_b, alpha, need_ctx):
    b, _, d = x.shape
    heads, n = r_k.shape
    zero = jnp.zeros((b, heads, n, n), F32)
    r_k2 = jnp.broadcast_to(r_k[None], (2, heads, n))

    def run(xs, m, s_f, s_b):
        r, k, v, z, lw, a = _rwkv_proj(xs, m[1], m[0], mu, w_rkvz, w0, w1, w2, a0, a1, a2)
        core = functools.partial(_rwkv_core, r, k, v, lw, a, k_k, k_a, r_k2)
        y_f, bv_f, s_f = core(s_f, reverse=False, direction=0)
        y_b, bv_b, s_b = core(s_b, reverse=True, direction=1)
        return (y_f, y_b, bv_f, bv_b, z), s_f, s_b

    def finish(xs, m, acts):
        specs = lambda tm: [_cols(tm, d, 0)] * 5
        return _out("rwkv", list(acts), specs, [gn_g.reshape(1, d), gn_b.reshape(1, d)], xs, m[2],
                    w_out.astype(BF16), ln_g, ln_b, alpha, width=n)

    acts_c, s_f, s_b = run(xc, mod_c, zero, zero)
    acts, _, _ = run(x, mod, s_f, s_b)
    x_new = finish(x, mod, acts)
    xc_new = finish(xc, mod_c, acts_c) if need_ctx else None
    return x_new, xc_new


def _rope_tables(n_tokens, rope):
    rows = n_tokens // GRID_W
    row = jnp.repeat(jnp.arange(rows, dtype=F32), GRID_W)
    col = jnp.tile(jnp.arange(GRID_W, dtype=F32), rows)
    n_freq = rope // 4
    inv_freq = ROPE_BASE ** (-jnp.arange(n_freq, dtype=F32) / n_freq)
    ang = jnp.concatenate([row[:, None] * inv_freq, col[:, None] * inv_freq], axis=-1)
    return jnp.cos(ang), jnp.sin(ang)


def _mla_layer(x, xc, mod, mod_c, w_in, q_norm, kv_norm, w_uq, w_ukv, w_out, ln_g, ln_b, alpha):
    b, l, d = x.shape
    lc = xc.shape[1]
    q_lora, kv_lora = q_norm.shape[0], kv_norm.shape[0]
    dv = LANES
    heads = w_out.shape[0] // dv
    rope = w_in.shape[1] - q_lora - kv_lora - heads * dv
    nope = w_uq.shape[1] // heads - rope
    cos, sin = _rope_tables(l, rope)
    proj = functools.partial(_mla_proj, w_in=w_in, q_norm=q_norm, kv_norm=kv_norm, w_uq=w_uq, w_ukv=w_ukv,
                             heads=heads, nope=nope, rope=rope, dv=dv)
    q, k, v, z = proj(x, mod[1], mod[0], cos=cos, sin=sin)
    ones, zeros = jnp.ones((lc, rope // 2), F32), jnp.zeros((lc, rope // 2), F32)
    _, kc, vc, _ = proj(xc, mod_c[1], mod_c[0], cos=ones, sin=zeros)
    o = _flash(q, jnp.concatenate([k, kc], axis=1), jnp.concatenate([v, vc], axis=1), heads=heads)
    specs = lambda tm: [_cols(tm, d, 0), _cols(tm, d, 0)]
    return _out("mla", [o, z], specs, [], x, mod[2], w_out.astype(BF16), ln_g, ln_b, alpha)


def kernel(x, c, ctx, c_ctx, ada_w, ada_b, ln_g, ln_b, gdn_w_in, gdn_conv, gdn_a_log, gdn_dt_bias, gdn_norm, gdn_w_out, rwkv_mu, rwkv_w_rkvz, rwkv_w0, rwkv_w1, rwkv_w2, rwkv_a0, rwkv_a1, rwkv_a2, rwkv_k_k, rwkv_k_a, rwkv_r_k, rwkv_gn_g, rwkv_gn_b, rwkv_w_out, gla_w_in, gla_w_g2, gla_b_g, gla_norm, gla_w_out, mla_w_in, mla_q_norm, mla_kv_norm, mla_w_uq, mla_w_ukv, mla_w_out):
    b, _, d = x.shape
    depth = ada_w.shape[0]
    n_mixers = 4
    assert depth == n_mixers, "one layer of each mixer; the last (MLA) layer needs no context output"
    alpha = (2.0 * depth) ** 0.25

    rows = -(-(b + 1) // SUBLANES) * SUBLANES
    cvec = jnp.concatenate([c, c_ctx[None], jnp.zeros((rows - b - 1, d), F32)], axis=0)
    mods = _ada(cvec, ada_w, ada_b)

    def split(i):
        lat = [mods[i, :b, j * d:(j + 1) * d].reshape(b, 1, d) for j in range(3)]
        con = [jnp.broadcast_to(mods[i, b, j * d:(j + 1) * d].reshape(1, 1, d), (b, 1, d)) for j in range(3)]
        return lat, con

    xc = ctx
    m, mc = split(0)
    x, xc = _gdn_layer(x, xc, m, mc, gdn_w_in[0], gdn_conv[0], gdn_a_log[0], gdn_dt_bias[0], gdn_norm[0],
                       gdn_w_out[0], ln_g[0], ln_b[0], alpha, True)
    m, mc = split(1)
    x, xc = _rwkv_layer(x, xc, m, mc, rwkv_mu[0], rwkv_w_rkvz[0], rwkv_w0[0], rwkv_w1[0], rwkv_w2[0], rwkv_a0[0],
                        rwkv_a1[0], rwkv_a2[0], rwkv_k_k[0], rwkv_k_a[0], rwkv_r_k[0], rwkv_gn_g[0], rwkv_gn_b[0],
                        rwkv_w_out[0], ln_g[1], ln_b[1], alpha, True)
    m, mc = split(2)
    x, xc = _gla_layer(x, xc, m, mc, gla_w_in[0], gla_w_g2[0], gla_b_g[0], gla_norm[0], gla_w_out[0],
                       ln_g[2], ln_b[2], alpha, True)
    m, mc = split(3)
    return _mla_layer(x, xc, m, mc, mla_w_in[0], mla_q_norm[0], mla_kv_norm[0], mla_w_uq[0], mla_w_ukv[0],
                      mla_w_out[0], ln_g[3], ln_b[3], alpha)
```

```python
import functools
import math

import jax
import jax.numpy as jnp
from jax import lax
from jax.experimental import pallas as pl
from jax.experimental.pallas import tpu as pltpu

F32 = jnp.float32
BF16 = jnp.bfloat16
HI = lax.Precision.HIGHEST

LANES = 128
SUBLANES = 8
VMEM_LIMIT = 56 * 1024 * 1024

CHUNK = 64
SUB = 16
LN_EPS = 1e-5
RMS_EPS = 1e-6
GN_EPS = 64e-5
GLA_TAU = 16.0
ROPE_BASE = 10000.0
GRID_W = 64
ROW_TILE = 512
SCAN_TILE = 512
FLASH_Q = 2048
FLASH_K = 3328
FLASH_ROWS = 256
GDN_HEADS_PER_STEP = 2


def _dot_hi(a, b):
    return jnp.dot(a, b, preferred_element_type=F32, precision=HI)


def _dot(a, b):
    return jnp.dot(a.astype(BF16), b.astype(BF16), preferred_element_type=F32)


def _bdot(a, b):
    return lax.dot_general(a.astype(BF16), b.astype(BF16), (((2,), (1,)), ((0,), (0,))),
                           preferred_element_type=F32)


def _bdot_tn(a, b):
    return lax.dot_general(a.astype(BF16), b.astype(BF16), (((1,), (1,)), ((0,), (0,))),
                           preferred_element_type=F32)


def _bdot_nt(a, b):
    return lax.dot_general(a.astype(BF16), b.astype(BF16), (((2,), (2,)), ((0,), (0,))),
                           preferred_element_type=F32)


def _dot_nt(a, b):
    return lax.dot_general(a.astype(BF16), b.astype(BF16), (((1,), (1,)), ((), ())),
                           preferred_element_type=F32)


def _silu(x):
    return x * jax.nn.sigmoid(x)


def _softplus(x):
    return jnp.maximum(x, 0.0) + jnp.log(1.0 + jnp.exp(-jnp.abs(x)))


def _params(*sem):
    return pltpu.CompilerParams(dimension_semantics=sem, vmem_limit_bytes=VMEM_LIMIT)


def _row_tile(n, cap):
    t = min(n, cap)
    assert n % t == 0 and t % SUBLANES == 0, (n, t)
    return t


def _col_tile(n, cap=768):
    assert n % LANES == 0, n
    best = LANES
    for t in range(LANES, cap + 1, LANES):
        if n % t == 0:
            best = t
    return best


def _masks(reverse, nc):
    r = lax.broadcasted_iota(jnp.int32, (nc, CHUNK, CHUNK), 1)
    c = lax.broadcasted_iota(jnp.int32, (nc, CHUNK, CHUNK), 2)
    if reverse:
        incl, strict = c >= r, c > r
    else:
        incl, strict = c <= r, c < r
    bd = (r // SUB) == (c // SUB)
    eye = (r == c).astype(F32)
    return incl, strict, bd, eye


def _tri_inverse(lm, bd, eye):
    d = jnp.where(bd, lm, 0.0)
    e = lm - d
    d2 = _bdot(d, d)
    d4 = _bdot(d2, d2)
    d8 = _bdot(d4, d4)
    t16 = _bdot(_bdot(eye - d, eye + d2), _bdot(eye + d4, eye + d8))
    f = _bdot(t16, e)
    f2 = _bdot(f, f)
    return _bdot(_bdot(eye - f, eye + f2), t16)


def _chunks(x):
    return x.reshape(x.shape[0] // CHUNK, CHUNK, x.shape[1])


def _cumsum_chunks(x, reverse):
    rows = x.shape[0]
    pos = lax.broadcasted_iota(jnp.int32, (rows, 1), 0) & (CHUNK - 1)
    k = 1
    while k < CHUNK:
        if reverse:
            x = x + jnp.where(pos < CHUNK - k, pltpu.roll(x, rows - k, 0), 0.0)
        else:
            x = x + jnp.where(pos >= k, pltpu.roll(x, k, 0), 0.0)
        k *= 2
    return x


def _chunk_order(nc, reverse):
    return range(nc - 1, -1, -1) if reverse else range(nc)


def _modulate(x, scale, shift):
    return x * (1.0 + scale) + shift


def _ada_kernel(c_ref, w_ref, b_ref, o_ref):
    o_ref[0] = _dot_hi(_silu(c_ref[...]), w_ref[0]) + b_ref[0]


def _ada(cvec, ada_w, ada_b):
    depth, d, n = ada_w.shape
    tn = _col_tile(n, 512)
    rows = cvec.shape[0]
    return pl.pallas_call(
        _ada_kernel,
        grid=(depth, n // tn),
        in_specs=[pl.BlockSpec((rows, d), lambda i, j: (0, 0)),
                  pl.BlockSpec((1, d, tn), lambda i, j: (i, 0, j)),
                  pl.BlockSpec((1, 1, tn), lambda i, j: (i, 0, j))],
        out_specs=pl.BlockSpec((1, rows, tn), lambda i, j: (i, 0, j)),
        out_shape=jax.ShapeDtypeStruct((depth, rows, n), F32),
        compiler_params=_params("arbitrary", "arbitrary"),
        name="ada",
    )(cvec, ada_w, ada_b.reshape(depth, 1, n))


def _proj_kernel(x_ref, sc_ref, sh_ref, w_ref, o_ref, *, tn):
    h = _modulate(x_ref[0], sc_ref[0], sh_ref[0]).astype(BF16)
    for j in range(w_ref.shape[1] // tn):
        o_ref[0, :, j * tn:(j + 1) * tn] = jnp.dot(h, w_ref[:, j * tn:(j + 1) * tn],
                                                   preferred_element_type=F32)


def _proj(x, scale, shift, w):
    b, l, d = x.shape
    n = w.shape[1]
    tm = _row_tile(l, ROW_TILE)
    vec = pl.BlockSpec((1, 1, d), lambda bi, i: (bi, 0, 0))
    return pl.pallas_call(
        functools.partial(_proj_kernel, tn=_col_tile(n)),
        grid=(b, l // tm),
        in_specs=[pl.BlockSpec((1, tm, d), lambda bi, i: (bi, i, 0)), vec, vec,
                  pl.BlockSpec((d, n), lambda bi, i: (0, 0))],
        out_specs=pl.BlockSpec((1, tm, n), lambda bi, i: (bi, i, 0)),
        out_shape=jax.ShapeDtypeStruct((b, l, n), F32),
        compiler_params=_params("arbitrary", "arbitrary"),
        name="proj",
    )(x, scale, shift, w)


def _group_rms(o, g, width):
    parts = []
    for j in range(o.shape[1] // width):
        oj = o[:, j * width:(j + 1) * width]
        parts.append(oj * lax.rsqrt(jnp.mean(oj * oj, -1, keepdims=True) + RMS_EPS) * g)
    return jnp.concatenate(parts, axis=1)


def _out_tail(pre, x_ref, gate_ref, w_ref, lg_ref, lb_ref, o_ref, alpha):
    y = jnp.dot(pre.astype(BF16), w_ref[...], preferred_element_type=F32)
    r = alpha * x_ref[0] + gate_ref[0] * y
    rc = r - jnp.mean(r, -1, keepdims=True)
    o_ref[0] = rc * lax.rsqrt(jnp.mean(rc * rc, -1, keepdims=True) + LN_EPS) * lg_ref[...] + lb_ref[...]


def _out_rms_kernel(of_ref, ob_ref, z_ref, g_ref, x_ref, gate_ref, w_ref, lg_ref, lb_ref, o_ref, *, alpha, width):
    o = _group_rms(of_ref[0] + ob_ref[0], g_ref[...], width)
    _out_tail(o * _silu(z_ref[0]), x_ref, gate_ref, w_ref, lg_ref, lb_ref, o_ref, alpha)


def _out_rwkv_kernel(yf_ref, yb_ref, bf_ref, bb_ref, z_ref, g_ref, gb_ref, x_ref, gate_ref, w_ref, lg_ref, lb_ref,
                     o_ref, *, alpha, width):
    y = yf_ref[0] + yb_ref[0]
    parts = []
    for j in range(y.shape[1] // width):
        yj = y[:, j * width:(j + 1) * width]
        yc = yj - jnp.mean(yj, -1, keepdims=True)
        parts.append(yc * lax.rsqrt(jnp.mean(yc * yc, -1, keepdims=True) + GN_EPS))
    yn = jnp.concatenate(parts, axis=1) * g_ref[...] + gb_ref[...]
    pre = (yn + bf_ref[0] + bb_ref[0]) * _silu(z_ref[0])
    _out_tail(pre, x_ref, gate_ref, w_ref, lg_ref, lb_ref, o_ref, alpha)


def _out_mla_kernel(o_in_ref, z_ref, x_ref, gate_ref, w_ref, lg_ref, lb_ref, o_ref, *, alpha):
    _out_tail(o_in_ref[0] * _silu(z_ref[0]), x_ref, gate_ref, w_ref, lg_ref, lb_ref, o_ref, alpha)


def _out(kind, acts, act_specs, vecs, x, gate, w_out, ln_g, ln_b, alpha, width=None):
    b, l, d = x.shape
    tm = _row_tile(l, ROW_TILE)
    k = w_out.shape[0]
    body = {"rms": functools.partial(_out_rms_kernel, alpha=alpha, width=width),
            "rwkv": functools.partial(_out_rwkv_kernel, alpha=alpha, width=width),
            "mla": functools.partial(_out_mla_kernel, alpha=alpha)}[kind]
    row = lambda n: pl.BlockSpec((1, n), lambda bi, i: (0, 0))
    in_specs = (list(act_specs(tm)) + [row(v.shape[1]) for v in vecs]
                + [pl.BlockSpec((1, tm, d), lambda bi, i: (bi, i, 0)),
                   pl.BlockSpec((1, 1, d), lambda bi, i: (bi, 0, 0)),
                   pl.BlockSpec((k, d), lambda bi, i: (0, 0)), row(d), row(d)])
    return pl.pallas_call(
        body,
        grid=(b, l // tm),
        in_specs=in_specs,
        out_specs=pl.BlockSpec((1, tm, d), lambda bi, i: (bi, i, 0)),
        out_shape=jax.ShapeDtypeStruct((b, l, d), F32),
        compiler_params=_params("arbitrary", "arbitrary"),
        name="out_" + kind,
    )(*acts, *vecs, x, gate, w_out, ln_g.reshape(1, d), ln_b.reshape(1, d))


def _cols(tm, width, col):
    return pl.BlockSpec((1, tm, width), lambda bi, i: (bi, i, col))


def _gdn_proj_kernel(x_ref, xp_ref, xn_ref, sc_ref, sh_ref, w_ref, cw_ref, o_ref, ext_ref, *, tn, conv_cols, taps):
    i = pl.program_id(1)
    last = pl.num_programs(1) - 1
    tm = x_ref.shape[1]
    pad = taps // 2
    n = w_ref.shape[1]
    sc, sh = sc_ref[0], sh_ref[0]
    h = _modulate(x_ref[0], sc, sh).astype(BF16)
    hp = _modulate(xp_ref[0, 0], sc, sh).astype(BF16)
    hn = _modulate(xn_ref[0, 0], sc, sh).astype(BF16)
    qk_cols = 2 * conv_cols // 3
    for j in range(conv_cols // tn):
        cols = slice(j * tn, (j + 1) * tn)
        w = w_ref[:, cols]
        ext_ref[0:SUBLANES, :] = jnp.where(i > 0, jnp.dot(hp, w, preferred_element_type=F32), 0.0)
        ext_ref[SUBLANES:SUBLANES + tm, :] = jnp.dot(h, w, preferred_element_type=F32)
        ext_ref[SUBLANES + tm:, :] = jnp.where(i < last, jnp.dot(hn, w, preferred_element_type=F32), 0.0)
        acc = jnp.zeros((tm, tn), F32)
        for t in range(taps):
            acc = acc + ext_ref[pl.ds(SUBLANES - pad + t, tm), :] * cw_ref[t:t + 1, cols]
        y = _silu(acc)
        for g in range(tn // LANES):
            yg = y[:, g * LANES:(g + 1) * LANES]
            c0 = j * tn + g * LANES
            if c0 < qk_cols:
                yg = yg * lax.rsqrt(jnp.sum(yg * yg, -1, keepdims=True) + RMS_EPS)
            if c0 < qk_cols // 2:
                yg = yg * (LANES ** -0.5)
            o_ref[0, :, c0:c0 + LANES] = yg
    for c0 in range(conv_cols, n, tn):
        c1 = min(c0 + tn, n)
        o_ref[0, :, c0:c1] = jnp.dot(h, w_ref[:, c0:c1], preferred_element_type=F32)


def _gdn_proj(x, scale, shift, w, conv_w, conv_cols):
    b, l, d = x.shape
    n = w.shape[1]
    tm = _row_tile(l, ROW_TILE)
    nblk = l // tm
    taps = conv_w.shape[0]
    tn = _col_tile(conv_cols)
    x4 = x.reshape(b, nblk, tm, d)
    zeros = jnp.zeros((b, 1, SUBLANES, d), x.dtype)
    xp = jnp.concatenate([zeros, x4[:, :-1, tm - SUBLANES:, :]], axis=1)
    xn = jnp.concatenate([x4[:, 1:, :SUBLANES, :], zeros], axis=1)
    conv_w = jnp.pad(conv_w, ((0, SUBLANES - taps), (0, 0)))
    vec = pl.BlockSpec((1, 1, d), lambda bi, i: (bi, 0, 0))
    halo = pl.BlockSpec((1, 1, SUBLANES, d), lambda bi, i: (bi, i, 0, 0))
    return pl.pallas_call(
        functools.partial(_gdn_proj_kernel, tn=tn, conv_cols=conv_cols, taps=taps),
        grid=(b, nblk),
        in_specs=[pl.BlockSpec((1, tm, d), lambda bi, i: (bi, i, 0)), halo, halo, vec, vec,
                  pl.BlockSpec((d, n), lambda bi, i: (0, 0)),
                  pl.BlockSpec((SUBLANES, conv_cols), lambda bi, i: (0, 0))],
        out_specs=pl.BlockSpec((1, tm, n), lambda bi, i: (bi, i, 0)),
        out_shape=jax.ShapeDtypeStruct((b, l, n), F32),
        scratch_shapes=[pltpu.VMEM((tm + 2 * SUBLANES, tn), F32)],
        compiler_params=_params("arbitrary", "arbitrary"),
        name="gdn_proj",
    )(x, xp, xn, scale, shift, w, conv_w)


def _gdn_kernel(q_ref, k_ref, v_ref, ab_ref, al_ref, dt_ref, s0_ref, o_ref, sf_ref, s_ref,
                *, reverse, direction, heads):
    hg = pl.program_id(1)
    i = pl.program_id(2)
    tb = q_ref.shape[1]
    nc = tb // CHUNK
    per = s_ref.shape[0]

    @pl.when(i == 0)
    def _():
        s_ref[...] = s0_ref[0]

    def stack(x):
        return jnp.concatenate([_chunks(x[:, hh * LANES:(hh + 1) * LANES]) for hh in range(per)], axis=0)

    q, k, v = stack(q_ref[0]), stack(k_ref[0]), stack(v_ref[0])
    ab = ab_ref[0]
    lane = lax.broadcasted_iota(jnp.int32, (1, LANES), 1)
    betas, gs, gcs = [], [], []
    for hh in range(per):
        col = direction * 2 * heads + hg * per + hh
        betas.append(jax.nn.sigmoid(jnp.sum(jnp.where(lane == col, ab, 0.0), -1, keepdims=True)))
        gpre = jnp.sum(jnp.where(lane == col + heads, ab, 0.0), -1, keepdims=True)
        g_h = -jnp.exp(al_ref[hh]) * _softplus(gpre + dt_ref[hh])
        gs.append(_chunks(g_h))
        gcs.append(_chunks(_cumsum_chunks(g_h, reverse)))
    beta = jnp.concatenate(betas, axis=0).reshape(per * nc, CHUNK, 1)
    g_b = jnp.concatenate(gs, axis=0)
    gc_b = jnp.concatenate(gcs, axis=0)

    incl, strict, bd, eye = _masks(reverse, per * nc)
    gtot = jnp.sum(g_b, axis=1, keepdims=True)
    m = gc_b[:, :, :CHUNK] - jnp.swapaxes(gc_b, 1, 2)[:, :CHUNK, :]
    decay = jnp.exp(jnp.where(incl, m, -jnp.inf))
    kb = k * beta
    lm = jnp.where(strict, _bdot_nt(kb, k) * decay, 0.0)
    t = _tri_inverse(lm, bd, eye)
    eg = jnp.exp(gc_b)
    u0 = _bdot(t, v * beta)
    wk = _bdot(t, kb * eg)
    qk = _bdot_nt(q, k) * decay
    kd = k * jnp.exp(gtot - gc_b)
    gl = jnp.exp(gtot)
    ms = _bdot_tn(kd, wk)
    c0 = _bdot_tn(kd, u0)
    qq = q * eg - _bdot(qk, wk)
    oc = _bdot(qk, u0)

    s = [s_ref[hh] for hh in range(per)]
    for c in _chunk_order(nc, reverse):
        for hh in range(per):
            j = hh * nc + c
            o_ref[0, c * CHUNK:(c + 1) * CHUNK, hh * LANES:(hh + 1) * LANES] = _dot(qq[j], s[hh]) + oc[j]
            s[hh] = s[hh] * gl[j] - _dot(ms[j], s[hh]) + c0[j]
    for hh in range(per):
        s_ref[hh] = s[hh]
        sf_ref[0, hh] = s[hh]


def _gdn_core(p, a_log, dt_bias, s0, *, reverse, direction, heads):
    b, l, _ = p.shape
    tb = _row_tile(l, SCAN_TILE)
    nblk = l // tb
    per = GDN_HEADS_PER_STEP
    assert heads % per == 0
    width = per * LANES
    blk = (lambda i: nblk - 1 - i) if reverse else (lambda i: i)
    al = jnp.broadcast_to(a_log[direction][:, None, None], (heads, 1, LANES))
    dt = jnp.broadcast_to(dt_bias[direction][:, None, None], (heads, 1, LANES))

    def act(off):
        return pl.BlockSpec((1, tb, width), lambda bi, h, i: (bi, blk(i), off // per + h))

    per_head = pl.BlockSpec((per, 1, LANES), lambda bi, h, i: (h, 0, 0))
    state = pl.BlockSpec((1, per, LANES, LANES), lambda bi, h, i: (bi, h, 0, 0))
    ab_col = (3 * heads * LANES + heads * LANES) // LANES
    return pl.pallas_call(
        functools.partial(_gdn_kernel, reverse=reverse, direction=direction, heads=heads),
        grid=(b, heads // per, nblk),
        in_specs=[act(0), act(heads), act(2 * heads),
                  pl.BlockSpec((1, tb, LANES), lambda bi, h, i: (bi, blk(i), ab_col)),
                  per_head, per_head, state],
        out_specs=[pl.BlockSpec((1, tb, width), lambda bi, h, i: (bi, blk(i), h)), state],
        out_shape=[jax.ShapeDtypeStruct((b, l, heads * LANES), F32),
                   jax.ShapeDtypeStruct(s0.shape, F32)],
        scratch_shapes=[pltpu.VMEM((per, LANES, LANES), F32)],
        compiler_params=_params("arbitrary", "arbitrary", "arbitrary"),
        name="gdn_bwd" if reverse else "gdn_fwd",
    )(p, p, p, p, al, dt, s0)


def _gla_kernel(q_ref, k_ref, v_ref, gr_ref, wg_ref, bg_ref, s0_ref, o_ref, sf_ref, s_ref, *, reverse):
    i = pl.program_id(2)
    tb = q_ref.shape[1]
    nc = tb // CHUNK

    @pl.when(i == 0)
    def _():
        s_ref[...] = s0_ref[0, 0]

    q = _chunks(q_ref[0] * (LANES ** -0.5))
    k = _chunks(k_ref[0])
    v = _chunks(v_ref[0])
    gpre = _dot_hi(gr_ref[0], wg_ref[0]) + bg_ref[0]
    g = -_softplus(-gpre) / GLA_TAU

    incl, _, _, _ = _masks(reverse, nc)
    bc = _chunks(_cumsum_chunks(g, reverse))
    btot = jnp.sum(_chunks(g), axis=1, keepdims=True)
    mid = CHUNK - 1 - CHUNK // 2 if reverse else CHUNK // 2
    ref = bc[:, mid:mid + 1, :]
    att = jnp.where(incl, _bdot_nt(q * jnp.exp(bc - ref), k * jnp.exp(ref - bc)), 0.0)
    o_intra = _bdot(att, v)
    qg = q * jnp.exp(bc)
    kv = _bdot_tn(v, k * jnp.exp(btot - bc))
    gl = jnp.exp(btot)

    st = s_ref[...]
    for c in _chunk_order(nc, reverse):
        o_ref[0, c * CHUNK:(c + 1) * CHUNK, :] = o_intra[c] + _dot_nt(qg[c], st)
        st = st * gl[c] + kv[c]
    s_ref[...] = st
    sf_ref[0, 0] = st


def _gla_core(p, wg, bg, s0, *, reverse, heads):
    b, l, _ = p.shape
    tb = _row_tile(l, SCAN_TILE)
    nblk = l // tb
    dk, dv = LANES, 2 * LANES
    blk = (lambda i: nblk - 1 - i) if reverse else (lambda i: i)
    gr_col = (2 * heads * dk + 2 * heads * dv) // LANES
    state = pl.BlockSpec((1, 1, dv, dk), lambda bi, h, i: (bi, h, 0, 0))
    return pl.pallas_call(
        functools.partial(_gla_kernel, reverse=reverse),
        grid=(b, heads, nblk),
        in_specs=[pl.BlockSpec((1, tb, dk), lambda bi, h, i: (bi, blk(i), h)),
                  pl.BlockSpec((1, tb, dk), lambda bi, h, i: (bi, blk(i), heads + h)),
                  pl.BlockSpec((1, tb, dv), lambda bi, h, i: (bi, blk(i), heads + h)),
                  pl.BlockSpec((1, tb, LANES), lambda bi, h, i: (bi, blk(i), gr_col)),
                  pl.BlockSpec((1, LANES, dk), lambda bi, h, i: (h, 0, 0)),
                  pl.BlockSpec((1, 1, dk), lambda bi, h, i: (h, 0, 0)),
                  state],
        out_specs=[pl.BlockSpec((1, tb, dv), lambda bi, h, i: (bi, blk(i), h)), state],
        out_shape=[jax.ShapeDtypeStruct((b, l, heads * dv), F32), jax.ShapeDtypeStruct(s0.shape, F32)],
        scratch_shapes=[pltpu.VMEM((dv, dk), F32)],
        compiler_params=_params("arbitrary", "arbitrary", "arbitrary"),
        name="gla_bwd" if reverse else "gla_fwd",
    )(p, p, p, p, wg, bg, s0)


def _rwkv_proj_kernel(x_ref, xp_ref, xn_ref, sc_ref, sh_ref, mu_ref, w4_ref, w1_ref, a1_ref, w2_ref, a2_ref,
                      w0_ref, a0_ref, r_ref, k_ref, v_ref, z_ref, lw_ref, a_ref):
    i = pl.program_id(1)
    last = pl.num_programs(1) - 1
    tm = x_ref.shape[1]
    rank = w2_ref.shape[1]
    sc, sh = sc_ref[0], sh_ref[0]
    u = _modulate(x_ref[0], sc, sh)
    up = jnp.where(i > 0, _modulate(xp_ref[0, 0, SUBLANES - 1:SUBLANES, :], sc, sh), 0.0)
    un = jnp.where(i < last, _modulate(xn_ref[0, 0, 0:1, :], sc, sh), 0.0)
    row = lax.broadcasted_iota(jnp.int32, (tm, 1), 0)
    prev = jnp.where(row == 0, up, pltpu.roll(u, 1, 0))
    nxt = jnp.where(row == tm - 1, un, pltpu.roll(u, tm - 1, 0))
    xx = 0.5 * (prev + nxt) - u

    def mix(j):
        return (u + xx * mu_ref[j:j + 1, :]).astype(BF16)

    r_ref[0] = jnp.dot(mix(0), w4_ref[0], preferred_element_type=F32)
    k_ref[0] = jnp.dot(mix(2), w4_ref[1], preferred_element_type=F32)
    v_ref[0] = jnp.dot(mix(3), w4_ref[2], preferred_element_type=F32)
    z_ref[0] = jnp.dot(mix(5), w4_ref[3], preferred_element_type=F32)
    lw1 = jnp.tanh(jnp.dot(mix(1), w1_ref[...], preferred_element_type=F32))
    a1 = jnp.dot(mix(4), a1_ref[...], preferred_element_type=F32)
    for n in range(2):
        wlog = w0_ref[n:n + 1, :] + _dot(lw1[:, n * rank:(n + 1) * rank], w2_ref[n])
        lw_ref[n, 0] = -jnp.exp(-_softplus(-wlog) - 0.5)
        a_ref[n, 0] = jax.nn.sigmoid(a0_ref[n:n + 1, :] + _dot(a1[:, n * rank:(n + 1) * rank], a2_ref[n]))


def _rwkv_proj(x, scale, shift, mu, w_rkvz, w0, w1, w2, a0, a1, a2):
    b, l, d = x.shape
    tm = _row_tile(l, 256)
    nblk = l // tm
    rank = w1.shape[2]
    x4 = x.reshape(b, nblk, tm, d)
    zeros = jnp.zeros((b, 1, SUBLANES, d), x.dtype)
    xp = jnp.concatenate([zeros, x4[:, :-1, tm - SUBLANES:, :]], axis=1)
    xn = jnp.concatenate([x4[:, 1:, :SUBLANES, :], zeros], axis=1)
    w1c = jnp.concatenate([w1[0], w1[1]], axis=1).astype(BF16)
    a1c = jnp.concatenate([a1[0], a1[1]], axis=1).astype(BF16)
    vec = pl.BlockSpec((1, 1, d), lambda bi, i: (bi, 0, 0))
    full = lambda a: pl.BlockSpec(a.shape, lambda bi, i: (0,) * a.ndim)
    halo = pl.BlockSpec((1, 1, SUBLANES, d), lambda bi, i: (bi, i, 0, 0))
    act = pl.BlockSpec((1, tm, d), lambda bi, i: (bi, i, 0))
    act2 = pl.BlockSpec((2, 1, tm, d), lambda bi, i: (0, bi, i, 0))
    w4 = w_rkvz.astype(BF16)
    w2b, a2b = w2.astype(BF16), a2.astype(BF16)
    sds = jax.ShapeDtypeStruct((b, l, d), F32)
    sds2 = jax.ShapeDtypeStruct((2, b, l, d), F32)
    return pl.pallas_call(
        _rwkv_proj_kernel,
        grid=(b, nblk),
        in_specs=[act, halo, halo, vec, vec, full(mu), full(w4), full(w1c), full(a1c), full(w2b), full(a2b),
                  full(w0), full(a0)],
        out_specs=[act, act, act, act, act2, act2],
        out_shape=[sds, sds, sds, sds, sds2, sds2],
        compiler_params=_params("arbitrary", "arbitrary"),
        name="rwkv_proj",
    )(x, xp, xn, scale, shift, mu, w4, w1c, a1c, w2b, a2b, w0, a0)


def _rwkv_kernel(r_ref, k_ref, v_ref, lw_ref, a_ref, kk_ref, ka_ref, rk_ref, s0_ref, y_ref, bv_ref, sf_ref, s_ref,
                 *, reverse, n):
    i = pl.program_id(2)
    tb = r_ref.shape[1]
    nc = tb // CHUNK
    per = LANES // n

    @pl.when(i == 0)
    def _():
        s_ref[...] = s0_ref[0]

    def heads(x):
        return jnp.concatenate([_chunks(x[:, hh * n:(hh + 1) * n]) for hh in range(per)], axis=0)

    r, kr, v, lw, a = r_ref[0], k_ref[0], v_ref[0], lw_ref[0, 0], a_ref[0, 0]
    kd = kr * (1.0 + (a - 1.0) * ka_ref[...])
    gc = _cumsum_chunks(lw, reverse)
    kq, kdh, rkd = heads(kr * kk_ref[...]), heads(kd), heads(r * kd * rk_ref[...])
    ah, rh, vh, lwc, gc = heads(a), heads(r), heads(v), heads(lw), heads(gc)
    kk = kq * lax.rsqrt(jnp.sum(kq * kq, -1, keepdims=True) + RMS_EPS)
    kka = kk * ah
    bv = jnp.sum(rkd, -1, keepdims=True) * vh
    gtot = jnp.sum(lwc, axis=1, keepdims=True)
    gcx = gc - lwc
    ref = 0.5 * gtot
    e_in = jnp.exp(ref - gc)
    e_out = jnp.exp(gtot - gc)
    gl = jnp.exp(gtot)

    incl, strict, bd, eye = _masks(reverse, per * nc)
    at = kk * jnp.exp(gcx - ref)
    bt = kka * e_in
    kt = kdh * e_in
    rt = rh * jnp.exp(gc - ref)
    a_ab = jnp.where(strict, _bdot_nt(at, bt), 0.0)
    a_ak = jnp.where(strict, _bdot_nt(at, kt), 0.0)
    a_rb = jnp.where(incl, _bdot_nt(rt, bt), 0.0)
    a_rk = jnp.where(incl, _bdot_nt(rt, kt), 0.0)
    t = _tri_inverse(a_ab, bd, eye)
    u0 = -_bdot(t, _bdot(a_ak, vh))
    w = _bdot(t, kk * jnp.exp(gcx))
    bh = kka * e_out
    ms = _bdot_tn(w, bh)
    c0 = _bdot_tn(u0, bh) + _bdot_tn(vh, kdh * e_out)
    rq = rh * jnp.exp(gc) - _bdot(a_rb, w)
    yc = _bdot(a_rb, u0) + _bdot(a_rk, vh)

    st = [s_ref[hh] for hh in range(per)]
    for c in _chunk_order(nc, reverse):
        for hh in range(per):
            j = hh * nc + c
            y_ref[0, c * CHUNK:(c + 1) * CHUNK, hh * n:(hh + 1) * n] = _dot_nt(rq[j], st[hh]) + yc[j]
            st[hh] = st[hh] * gl[j] - _dot(st[hh], ms[j]) + c0[j]
    for hh in range(per):
        bv_ref[0, :, hh * n:(hh + 1) * n] = bv[hh * nc:(hh + 1) * nc].reshape(tb, n)
        s_ref[hh] = st[hh]
        sf_ref[0, hh] = st[hh]


def _rwkv_core(r, k, v, lw, a, k_k, k_a, r_k, s0, *, reverse, direction):
    b, l, d = r.shape
    n = s0.shape[-1]
    per = LANES // n
    tb = _row_tile(l, SCAN_TILE)
    nblk = l // tb
    blk = (lambda i: nblk - 1 - i) if reverse else (lambda i: i)
    act = pl.BlockSpec((1, tb, LANES), lambda bi, h, i: (bi, blk(i), h))
    act2 = pl.BlockSpec((1, 1, tb, LANES), lambda bi, h, i: (direction, bi, blk(i), h))
    vec = pl.BlockSpec((1, LANES), lambda bi, h, i: (0, h))
    state = pl.BlockSpec((1, per, n, n), lambda bi, h, i: (bi, h, 0, 0))
    sds = jax.ShapeDtypeStruct((b, l, d), F32)
    return pl.pallas_call(
        functools.partial(_rwkv_kernel, reverse=reverse, n=n),
        grid=(b, d // LANES, nblk),
        in_specs=[act, act, act, act2, act2, vec, vec, vec, state],
        out_specs=[act, act, state],
        out_shape=[sds, sds, jax.ShapeDtypeStruct(s0.shape, F32)],
        scratch_shapes=[pltpu.VMEM((per, n, n), F32)],
        compiler_params=_params("arbitrary", "arbitrary", "arbitrary"),
        name="rwkv_bwd" if reverse else "rwkv_fwd",
    )(r, k, v, lw, a, k_k.reshape(1, d), k_a.reshape(1, d), r_k[direction].reshape(1, d), s0)


def _mla_proj_kernel(x_ref, sc_ref, sh_ref, win_ref, qn_ref, kvn_ref, wq_ref, wk_ref, wv_ref, cq_ref, s1_ref, s2_ref,
                     q_ref, k_ref, v_ref, z_ref, *, heads, q_lora, kv_lora, scale):
    d = z_ref.shape[2]
    h = _modulate(x_ref[0], sc_ref[0], sh_ref[0]).astype(BF16)
    p = jnp.dot(h, win_ref[...], preferred_element_type=F32)

    def rms(t, g):
        return t * lax.rsqrt(jnp.mean(t * t, -1, keepdims=True) + RMS_EPS) * g

    def rope(t):
        return (t * cq_ref[...] + pltpu.roll(t, LANES - LANES // 4, 1) * s1_ref[...]
                + pltpu.roll(t, LANES // 4, 1) * s2_ref[...])

    ql = rms(p[:, :q_lora], qn_ref[...]).astype(BF16)
    kvl = rms(p[:, q_lora:q_lora + kv_lora], kvn_ref[...]).astype(BF16)
    z_ref[0] = p[:, q_lora + kv_lora:q_lora + kv_lora + d]
    kr = rope(p[:, q_lora + kv_lora + d:])
    q = jnp.dot(ql, wq_ref[...], preferred_element_type=F32) * scale
    kn = jnp.dot(kvl, wk_ref[...], preferred_element_type=F32)
    v_ref[0] = jnp.dot(kvl, wv_ref[...], preferred_element_type=F32).astype(BF16)
    for j in range(heads):
        base = 2 * LANES * j
        q_ref[0, :, base:base + LANES] = q[:, base:base + LANES].astype(BF16)
        q_ref[0, :, base + LANES:base + 2 * LANES] = rope(q[:, base + LANES:base + 2 * LANES]).astype(BF16)
        k_ref[0, :, base:base + LANES] = kn[:, j * LANES:(j + 1) * LANES].astype(BF16)
        k_ref[0, :, base + LANES:base + 2 * LANES] = kr.astype(BF16)


def _mla_proj(x, scale, shift, w_in, q_norm, kv_norm, w_uq, w_ukv, cos, sin, *, heads, nope, rope, dv):
    b, l, d = x.shape
    q_lora, kv_lora = q_norm.shape[0], kv_norm.shape[0]
    tm = _row_tile(l, 256)
    half = rope // 2
    assert nope == LANES and dv == LANES and rope == LANES // 2
    o2 = q_lora + kv_lora
    w_in_r = jnp.concatenate([w_in[:, :o2], w_in[:, o2 + rope:], w_in[:, o2:o2 + rope],
                              jnp.zeros((d, LANES - rope), w_in.dtype)], axis=1).astype(BF16)
    wq = w_uq.reshape(q_lora, heads, nope + rope)
    wq = jnp.concatenate([wq, jnp.zeros((q_lora, heads, LANES - rope), w_uq.dtype)], axis=2)
    wq = wq.reshape(q_lora, heads * 2 * LANES).astype(BF16)
    wkv = w_ukv.reshape(kv_lora, heads, nope + dv)
    wk = wkv[:, :, :nope].reshape(kv_lora, heads * nope).astype(BF16)
    wv = wkv[:, :, nope:].reshape(kv_lora, heads * dv).astype(BF16)
    zer = jnp.zeros((l, half), F32)
    zer2 = jnp.zeros((l, LANES - rope), F32)
    cq = jnp.concatenate([cos, cos, zer2], axis=1)
    s1 = jnp.concatenate([-sin, zer, zer2], axis=1)
    s2 = jnp.concatenate([zer, sin, zer2], axis=1)
    vec = pl.BlockSpec((1, 1, d), lambda bi, i: (bi, 0, 0))
    full = lambda a: pl.BlockSpec(a.shape, lambda bi, i: (0,) * a.ndim)
    tab = pl.BlockSpec((tm, LANES), lambda bi, i: (i, 0))
    row = lambda w: pl.BlockSpec((1, tm, w), lambda bi, i: (bi, i, 0))
    qn = q_norm.reshape(1, q_lora)
    kvn = kv_norm.reshape(1, kv_lora)
    return pl.pallas_call(
        functools.partial(_mla_proj_kernel, heads=heads, q_lora=q_lora, kv_lora=kv_lora,
                          scale=(nope + rope) ** -0.5 * math.log2(math.e)),
        grid=(b, l // tm),
        in_specs=[row(d), vec, vec, full(w_in_r), full(qn), full(kvn), full(wq), full(wk), full(wv), tab, tab, tab],
        out_specs=[row(heads * 2 * LANES), row(heads * 2 * LANES), row(heads * dv), row(d)],
        out_shape=[jax.ShapeDtypeStruct((b, l, heads * 2 * LANES), BF16),
                   jax.ShapeDtypeStruct((b, l, heads * 2 * LANES), BF16),
                   jax.ShapeDtypeStruct((b, l, heads * dv), BF16),
                   jax.ShapeDtypeStruct((b, l, d), F32)],
        compiler_params=_params("arbitrary", "arbitrary"),
        name="mla_proj",
    )(x, scale, shift, w_in_r, qn, kvn, wq, wk, wv, cq, s1, s2)


def _flash_kernel(q_ref, k_ref, v_ref, o_ref, m_ref, acc_ref):
    j = pl.program_id(3)

    @pl.when(j == 0)
    def _():
        m_ref[...] = jnp.full(m_ref.shape, -jnp.inf, F32)
        acc_ref[...] = jnp.zeros(acc_ref.shape, F32)

    groups = q_ref.shape[1] // FLASH_ROWS

    def scores(g):
        return lax.dot_general(q_ref[0, g * FLASH_ROWS:(g + 1) * FLASH_ROWS, :], k_ref[0],
                               (((1,), (1,)), ((), ())), preferred_element_type=F32)

    v1 = jnp.concatenate([v_ref[0], jnp.ones(v_ref.shape[1:], BF16)], axis=1)
    s_next = scores(0)
    for g in range(groups):
        rows = slice(g * FLASH_ROWS, (g + 1) * FLASH_ROWS)
        s = s_next
        if g + 1 < groups:
            s_next = scores(g + 1)
        m_old = m_ref[rows, :]
        m_new = jnp.maximum(m_old, jnp.max(s, -1, keepdims=True))
        alpha = jnp.exp2(m_old - m_new)
        p = jnp.exp2(s - m_new)
        acc_ref[rows, :] = alpha * acc_ref[rows, :] + jnp.dot(p.astype(BF16), v1, preferred_element_type=F32)
        m_ref[rows, :] = m_new

    @pl.when(j == pl.num_programs(3) - 1)
    def _():
        o_ref[0] = acc_ref[:, :LANES] / acc_ref[:, LANES:]


def _flash(q, k, v, *, heads):
    b, l, _ = q.shape
    lk = k.shape[1]
    tq = _row_tile(l, FLASH_Q)
    tk = LANES
    for t in range(LANES, FLASH_K + 1, LANES):
        if lk % t == 0:
            tk = t
    return pl.pallas_call(
        _flash_kernel,
        grid=(b, heads, l // tq, lk // tk),
        in_specs=[pl.BlockSpec((1, tq, 2 * LANES), lambda bi, h, i, j: (bi, i, h)),
                  pl.BlockSpec((1, tk, 2 * LANES), lambda bi, h, i, j: (bi, j, h)),
                  pl.BlockSpec((1, tk, LANES), lambda bi, h, i, j: (bi, j, h))],
        out_specs=pl.BlockSpec((1, tq, LANES), lambda bi, h, i, j: (bi, i, h)),
        out_shape=jax.ShapeDtypeStruct((b, l, heads * LANES), F32),
        scratch_shapes=[pltpu.VMEM((tq, 1), F32), pltpu.VMEM((tq, 2 * LANES), F32)],
        compiler_params=_params("arbitrary", "arbitrary", "arbitrary", "arbitrary"),
        name="flash",
    )(q, k, v)


def _pad_cols(w):
    n = w.shape[1]
    return jnp.pad(w, ((0, 0), (0, -n % LANES))).astype(BF16)


def _gdn_layer(x, xc, mod, mod_c, w_in, conv_w, a_log, dt_bias, norm_g, w_out, ln_g, ln_b, alpha, need_ctx):
    heads = a_log.shape[1]
    dk = dv = norm_g.shape[0]
    assert dk == LANES and conv_w.shape[1] == 3 * heads * dk
    b = x.shape[0]
    w = _pad_cols(w_in)
    zero = jnp.zeros((b, heads, dk, dv), F32)

    def run(xs, m, s_f, s_b):
        p = _gdn_proj(xs, m[1], m[0], w, conv_w, 3 * heads * dk)
        core = functools.partial(_gdn_core, p, a_log, dt_bias, heads=heads)
        o_f, s_f = core(s_f, reverse=False, direction=0)
        o_b, s_b = core(s_b, reverse=True, direction=1)
        return p, o_f, o_b, s_f, s_b

    def finish(xs, m, p, o_f, o_b):
        specs = lambda tm: [_cols(tm, heads * dv, 0), _cols(tm, heads * dv, 0), _cols(tm, heads * dv, 3)]
        return _out("rms", [o_f, o_b, p], specs, [norm_g.reshape(1, dv)], xs, m[2], w_out.astype(BF16),
                    ln_g, ln_b, alpha, width=dv)

    pc, oc_f, oc_b, s_f, s_b = run(xc, mod_c, zero, zero)
    p, o_f, o_b, _, _ = run(x, mod, s_f, s_b)
    x_new = finish(x, mod, p, o_f, o_b)
    xc_new = finish(xc, mod_c, pc, oc_f, oc_b) if need_ctx else None
    return x_new, xc_new


def _gla_layer(x, xc, mod, mod_c, w_in, w_g2, b_g, norm_g, w_out, ln_g, ln_b, alpha, need_ctx):
    dv = norm_g.shape[0]
    rank, qk = w_g2.shape[1], w_g2.shape[2]
    dk = LANES
    heads = qk // dk
    assert dv == 2 * LANES and w_in.shape[1] == 2 * qk + 2 * heads * dv + 2 * rank
    b = x.shape[0]
    w = _pad_cols(w_in)
    zero = jnp.zeros((b, heads, dv, dk), F32)

    def gate_w(direction):
        wg = jnp.zeros((LANES, qk), F32).at[direction * rank:(direction + 1) * rank].set(w_g2[direction])
        return wg.reshape(LANES, heads, dk).transpose(1, 0, 2), b_g[direction].reshape(heads, 1, dk)

    def run(xs, m, s_f, s_b):
        p = _proj(xs, m[1], m[0], w)
        o_f, s_f = _gla_core(p, *gate_w(0), s_f, reverse=False, heads=heads)
        o_b, s_b = _gla_core(p, *gate_w(1), s_b, reverse=True, heads=heads)
        return p, o_f, o_b, s_f, s_b

    def finish(xs, m, p, o_f, o_b):
        specs = lambda tm: [_cols(tm, heads * dv, 0), _cols(tm, heads * dv, 0), _cols(tm, heads * dv, 2)]
        return _out("rms", [o_f, o_b, p], specs, [norm_g.reshape(1, dv)], xs, m[2], w_out.astype(BF16),
                    ln_g, ln_b, alpha, width=dv)

    pc, oc_f, oc_b, s_f, s_b = run(xc, mod_c, zero, zero)
    p, o_f, o_b, _, _ = run(x, mod, s_f, s_b)
    x_new = finish(x, mod, p, o_f, o_b)
    xc_new = finish(xc, mod_c, pc, oc_f, oc_b) if need_ctx else None
    return x_new, xc_new


def _rwkv_layer(x, xc, mod, mod_c, mu, w_rkvz, w0, w1, w2, a0, a1, a2, k_k, k_a, r_k, gn_g, gn_b, w_out,
                ln_g, ln_b, alpha, need_ctx):
    b, _, d = x.shape
    heads, n = r_k.shape
    zero = jnp.zeros((b, heads, n, n), F32)
    r_k2 = jnp.broadcast_to(r_k[None], (2, heads, n))

    def run(xs, m, s_f, s_b):
        r, k, v, z, lw, a = _rwkv_proj(xs, m[1], m[0], mu, w_rkvz, w0, w1, w2, a0, a1, a2)
        core = functools.partial(_rwkv_core, r, k, v, lw, a, k_k, k_a, r_k2)
        y_f, bv_f, s_f = core(s_f, reverse=False, direction=0)
        y_b, bv_b, s_b = core(s_b, reverse=True, direction=1)
        return (y_f, y_b, bv_f, bv_b, z), s_f, s_b

    def finish(xs, m, acts):
        specs = lambda tm: [_cols(tm, d, 0)] * 5
        return _out("rwkv", list(acts), specs, [gn_g.reshape(1, d), gn_b.reshape(1, d)], xs, m[2],
                    w_out.astype(BF16), ln_g, ln_b, alpha, width=n)

    acts_c, s_f, s_b = run(xc, mod_c, zero, zero)
    acts, _, _ = run(x, mod, s_f, s_b)
    x_new = finish(x, mod, acts)
    xc_new = finish(xc, mod_c, acts_c) if need_ctx else None
    return x_new, xc_new


def _rope_tables(n_tokens, rope):
    rows = n_tokens // GRID_W
    row = jnp.repeat(jnp.arange(rows, dtype=F32), GRID_W)
    col = jnp.tile(jnp.arange(GRID_W, dtype=F32), rows)
    n_freq = rope // 4
    inv_freq = ROPE_BASE ** (-jnp.arange(n_freq, dtype=F32) / n_freq)
    ang = jnp.concatenate([row[:, None] * inv_freq, col[:, None] * inv_freq], axis=-1)
    return jnp.cos(ang), jnp.sin(ang)


def _mla_layer(x, xc, mod, mod_c, w_in, q_norm, kv_norm, w_uq, w_ukv, w_out, ln_g, ln_b, alpha):
    b, l, d = x.shape
    lc = xc.shape[1]
    q_lora, kv_lora = q_norm.shape[0], kv_norm.shape[0]
    dv = LANES
    heads = w_out.shape[0] // dv
    rope = w_in.shape[1] - q_lora - kv_lora - heads * dv
    nope = w_uq.shape[1] // heads - rope
    cos, sin = _rope_tables(l, rope)
    proj = functools.partial(_mla_proj, w_in=w_in, q_norm=q_norm, kv_norm=kv_norm, w_uq=w_uq, w_ukv=w_ukv,
                             heads=heads, nope=nope, rope=rope, dv=dv)
    q, k, v, z = proj(x, mod[1], mod[0], cos=cos, sin=sin)
    ones, zeros = jnp.ones((lc, rope // 2), F32), jnp.zeros((lc, rope // 2), F32)
    _, kc, vc, _ = proj(xc, mod_c[1], mod_c[0], cos=ones, sin=zeros)
    o = _flash(q, jnp.concatenate([k, kc], axis=1), jnp.concatenate([v, vc], axis=1), heads=heads)
    specs = lambda tm: [_cols(tm, d, 0), _cols(tm, d, 0)]
    return _out("mla", [o, z], specs, [], x, mod[2], w_out.astype(BF16), ln_g, ln_b, alpha)


def kernel(x, c, ctx, c_ctx, ada_w, ada_b, ln_g, ln_b, gdn_w_in, gdn_conv, gdn_a_log, gdn_dt_bias, gdn_norm, gdn_w_out, rwkv_mu, rwkv_w_rkvz, rwkv_w0, rwkv_w1, rwkv_w2, rwkv_a0, rwkv_a1, rwkv_a2, rwkv_k_k, rwkv_k_a, rwkv_r_k, rwkv_gn_g, rwkv_gn_b, rwkv_w_out, gla_w_in, gla_w_g2, gla_b_g, gla_norm, gla_w_out, mla_w_in, mla_q_norm, mla_kv_norm, mla_w_uq, mla_w_ukv, mla_w_out):
    b, _, d = x.shape
    depth = ada_w.shape[0]
    n_mixers = 4
    assert depth == n_mixers, "one layer of each mixer; the last (MLA) layer needs no context output"
    alpha = (2.0 * depth) ** 0.25

    rows = -(-(b + 1) // SUBLANES) * SUBLANES
    cvec = jnp.concatenate([c, c_ctx[None], jnp.zeros((rows - b - 1, d), F32)], axis=0)
    mods = _ada(cvec, ada_w, ada_b)

    def split(i):
        lat = [mods[i, :b, j * d:(j + 1) * d].reshape(b, 1, d) for j in range(3)]
        con = [jnp.broadcast_to(mods[i, b, j * d:(j + 1) * d].reshape(1, 1, d), (b, 1, d)) for j in range(3)]
        return lat, con

    xc = ctx
    m, mc = split(0)
    x, xc = _gdn_layer(x, xc, m, mc, gdn_w_in[0], gdn_conv[0], gdn_a_log[0], gdn_dt_bias[0], gdn_norm[0],
                       gdn_w_out[0], ln_g[0], ln_b[0], alpha, True)
    m, mc = split(1)
    x, xc = _rwkv_layer(x, xc, m, mc, rwkv_mu[0], rwkv_w_rkvz[0], rwkv_w0[0], rwkv_w1[0], rwkv_w2[0], rwkv_a0[0],
                        rwkv_a1[0], rwkv_a2[0], rwkv_k_k[0], rwkv_k_a[0], rwkv_r_k[0], rwkv_gn_g[0], rwkv_gn_b[0],
                        rwkv_w_out[0], ln_g[1], ln_b[1], alpha, True)
    m, mc = split(2)
    x, xc = _gla_layer(x, xc, m, mc, gla_w_in[0], gla_w_g2[0], gla_b_g[0], gla_norm[0], gla_w_out[0],
                       ln_g[2], ln_b[2], alpha, True)
    m, mc = split(3)
    return _mla_layer(x, xc, m, mc, mla_w_in[0], mla_q_norm[0], mla_kv_norm[0], mla_w_uq[0], mla_w_ukv[0],
                      mla_w_out[0], ln_g[3], ln_b[3], alpha)
```

```python
import functools
import math

import jax
import jax.numpy as jnp
from jax import lax
from jax.experimental import pallas as pl
from jax.experimental.pallas import tpu as pltpu

F32 = jnp.float32
BF16 = jnp.bfloat16
HI = lax.Precision.HIGHEST

LANES = 128
SUBLANES = 8
VMEM_LIMIT = 56 * 1024 * 1024

CHUNK = 64
SUB = 16
LN_EPS = 1e-5
RMS_EPS = 1e-6
GN_EPS = 64e-5
GLA_TAU = 16.0
ROPE_BASE = 10000.0
GRID_W = 64
ROW_TILE = 512
SCAN_TILE = 512
FLASH_Q = 2048
FLASH_K = 3328
FLASH_ROWS = 256
GDN_HEADS_PER_STEP = 2


def _dot_hi(a, b):
    return jnp.dot(a, b, preferred_element_type=F32, precision=HI)


def _dot(a, b):
    return jnp.dot(a.astype(BF16), b.astype(BF16), preferred_element_type=F32)


def _bdot(a, b):
    return lax.dot_general(a.astype(BF16), b.astype(BF16), (((2,), (1,)), ((0,), (0,))),
                           preferred_element_type=F32)


def _bdot_tn(a, b):
    return lax.dot_general(a.astype(BF16), b.astype(BF16), (((1,), (1,)), ((0,), (0,))),
                           preferred_element_type=F32)


def _bdot_nt(a, b):
    return lax.dot_general(a.astype(BF16), b.astype(BF16), (((2,), (2,)), ((0,), (0,))),
                           preferred_element_type=F32)


def _dot_nt(a, b):
    return lax.dot_general(a.astype(BF16), b.astype(BF16), (((1,), (1,)), ((), ())),
                           preferred_element_type=F32)


def _silu(x):
    return x * jax.nn.sigmoid(x)


def _softplus(x):
    return jnp.maximum(x, 0.0) + jnp.log(1.0 + jnp.exp(-jnp.abs(x)))


def _params(*sem):
    return pltpu.CompilerParams(dimension_semantics=sem, vmem_limit_bytes=VMEM_LIMIT)


def _row_tile(n, cap):
    t = min(n, cap)
    assert n % t == 0 and t % SUBLANES == 0, (n, t)
    return t


def _col_tile(n, cap=768):
    assert n % LANES == 0, n
    best = LANES
    for t in range(LANES, cap + 1, LANES):
        if n % t == 0:
            best = t
    return best


def _masks(reverse, nc):
    r = lax.broadcasted_iota(jnp.int32, (nc, CHUNK, CHUNK), 1)
    c = lax.broadcasted_iota(jnp.int32, (nc, CHUNK, CHUNK), 2)
    if reverse:
        incl, strict = c >= r, c > r
    else:
        incl, strict = c <= r, c < r
    bd = (r // SUB) == (c // SUB)
    eye = (r == c).astype(F32)
    return incl, strict, bd, eye


def _tri_inverse(lm, bd, eye, tick=lambda: None):
    d = jnp.where(bd, lm, 0.0)
    e = lm - d
    d2 = _bdot(d, d)
    tick()
    d4 = _bdot(d2, d2)
    tick()
    d8 = _bdot(d4, d4)
    tick()
    t16 = _bdot(_bdot(eye - d, eye + d2), _bdot(eye + d4, eye + d8))
    tick()
    f = _bdot(t16, e)
    tick()
    f2 = _bdot(f, f)
    tick()
    return _bdot(_bdot(eye - f, eye + f2), t16)


def _chunks(x):
    return x.reshape(x.shape[0] // CHUNK, CHUNK, x.shape[1])


def _cumsum_chunks(x, reverse):
    rows = x.shape[0]
    pos = lax.broadcasted_iota(jnp.int32, (rows, 1), 0) & (CHUNK - 1)
    k = 1
    while k < CHUNK:
        if reverse:
            x = x + jnp.where(pos < CHUNK - k, pltpu.roll(x, rows - k, 0), 0.0)
        else:
            x = x + jnp.where(pos >= k, pltpu.roll(x, k, 0), 0.0)
        k *= 2
    return x


def _chunk_order(nc, reverse):
    return range(nc - 1, -1, -1) if reverse else range(nc)


def _lagged_blocks(nblk, reverse):
    order = (lambda j: nblk - 1 - j) if reverse else (lambda j: j)
    return (lambda i: order(jnp.minimum(i, nblk - 1))), (lambda i: order(jnp.maximum(i - 1, 0)))


def _modulate(x, scale, shift):
    return x * (1.0 + scale) + shift


def _ada_kernel(c_ref, w_ref, b_ref, o_ref):
    o_ref[0] = _dot_hi(_silu(c_ref[...]), w_ref[0]) + b_ref[0]


def _ada(cvec, ada_w, ada_b):
    depth, d, n = ada_w.shape
    tn = _col_tile(n, 512)
    rows = cvec.shape[0]
    return pl.pallas_call(
        _ada_kernel,
        grid=(depth, n // tn),
        in_specs=[pl.BlockSpec((rows, d), lambda i, j: (0, 0)),
                  pl.BlockSpec((1, d, tn), lambda i, j: (i, 0, j)),
                  pl.BlockSpec((1, 1, tn), lambda i, j: (i, 0, j))],
        out_specs=pl.BlockSpec((1, rows, tn), lambda i, j: (i, 0, j)),
        out_shape=jax.ShapeDtypeStruct((depth, rows, n), F32),
        compiler_params=_params("arbitrary", "arbitrary"),
        name="ada",
    )(cvec, ada_w, ada_b.reshape(depth, 1, n))


def _proj_kernel(x_ref, sc_ref, sh_ref, w_ref, o_ref, *, tn):
    h = _modulate(x_ref[0], sc_ref[0], sh_ref[0]).astype(BF16)
    for j in range(w_ref.shape[1] // tn):
        o_ref[0, :, j * tn:(j + 1) * tn] = jnp.dot(h, w_ref[:, j * tn:(j + 1) * tn],
                                                   preferred_element_type=F32)


def _proj(x, scale, shift, w):
    b, l, d = x.shape
    n = w.shape[1]
    tm = _row_tile(l, ROW_TILE)
    vec = pl.BlockSpec((1, 1, d), lambda bi, i: (bi, 0, 0))
    return pl.pallas_call(
        functools.partial(_proj_kernel, tn=_col_tile(n)),
        grid=(b, l // tm),
        in_specs=[pl.BlockSpec((1, tm, d), lambda bi, i: (bi, i, 0)), vec, vec,
                  pl.BlockSpec((d, n), lambda bi, i: (0, 0))],
        out_specs=pl.BlockSpec((1, tm, n), lambda bi, i: (bi, i, 0)),
        out_shape=jax.ShapeDtypeStruct((b, l, n), F32),
        compiler_params=_params("arbitrary", "arbitrary"),
        name="proj",
    )(x, scale, shift, w)


def _group_rms(o, g, width):
    parts = []
    for j in range(o.shape[1] // width):
        oj = o[:, j * width:(j + 1) * width]
        parts.append(oj * lax.rsqrt(jnp.mean(oj * oj, -1, keepdims=True) + RMS_EPS) * g)
    return jnp.concatenate(parts, axis=1)


def _out_tail(pre, x_ref, gate_ref, w_ref, lg_ref, lb_ref, o_ref, alpha):
    y = jnp.dot(pre.astype(BF16), w_ref[...], preferred_element_type=F32)
    r = alpha * x_ref[0] + gate_ref[0] * y
    rc = r - jnp.mean(r, -1, keepdims=True)
    o_ref[0] = rc * lax.rsqrt(jnp.mean(rc * rc, -1, keepdims=True) + LN_EPS) * lg_ref[...] + lb_ref[...]


def _out_rms_kernel(of_ref, ob_ref, z_ref, g_ref, x_ref, gate_ref, w_ref, lg_ref, lb_ref, o_ref, *, alpha, width):
    o = _group_rms(of_ref[0] + ob_ref[0], g_ref[...], width)
    _out_tail(o * _silu(z_ref[0]), x_ref, gate_ref, w_ref, lg_ref, lb_ref, o_ref, alpha)


def _out_rwkv_kernel(yf_ref, yb_ref, bf_ref, bb_ref, z_ref, g_ref, gb_ref, x_ref, gate_ref, w_ref, lg_ref, lb_ref,
                     o_ref, *, alpha, width):
    y = yf_ref[0] + yb_ref[0]
    parts = []
    for j in range(y.shape[1] // width):
        yj = y[:, j * width:(j + 1) * width]
        yc = yj - jnp.mean(yj, -1, keepdims=True)
        parts.append(yc * lax.rsqrt(jnp.mean(yc * yc, -1, keepdims=True) + GN_EPS))
    yn = jnp.concatenate(parts, axis=1) * g_ref[...] + gb_ref[...]
    pre = (yn + bf_ref[0] + bb_ref[0]) * _silu(z_ref[0])
    _out_tail(pre, x_ref, gate_ref, w_ref, lg_ref, lb_ref, o_ref, alpha)


def _out_mla_kernel(o_in_ref, z_ref, x_ref, gate_ref, w_ref, lg_ref, lb_ref, o_ref, *, alpha):
    _out_tail(o_in_ref[0] * _silu(z_ref[0]), x_ref, gate_ref, w_ref, lg_ref, lb_ref, o_ref, alpha)


def _out(kind, acts, act_specs, vecs, x, gate, w_out, ln_g, ln_b, alpha, width=None):
    b, l, d = x.shape
    tm = _row_tile(l, ROW_TILE)
    k = w_out.shape[0]
    body = {"rms": functools.partial(_out_rms_kernel, alpha=alpha, width=width),
            "rwkv": functools.partial(_out_rwkv_kernel, alpha=alpha, width=width),
            "mla": functools.partial(_out_mla_kernel, alpha=alpha)}[kind]
    row = lambda n: pl.BlockSpec((1, n), lambda bi, i: (0, 0))
    in_specs = (list(act_specs(tm)) + [row(v.shape[1]) for v in vecs]
                + [pl.BlockSpec((1, tm, d), lambda bi, i: (bi, i, 0)),
                   pl.BlockSpec((1, 1, d), lambda bi, i: (bi, 0, 0)),
                   pl.BlockSpec((k, d), lambda bi, i: (0, 0)), row(d), row(d)])
    return pl.pallas_call(
        body,
        grid=(b, l // tm),
        in_specs=in_specs,
        out_specs=pl.BlockSpec((1, tm, d), lambda bi, i: (bi, i, 0)),
        out_shape=jax.ShapeDtypeStruct((b, l, d), F32),
        compiler_params=_params("arbitrary", "arbitrary"),
        name="out_" + kind,
    )(*acts, *vecs, x, gate, w_out, ln_g.reshape(1, d), ln_b.reshape(1, d))


def _cols(tm, width, col):
    return pl.BlockSpec((1, tm, width), lambda bi, i: (bi, i, col))


def _gdn_proj_kernel(x_ref, xp_ref, xn_ref, sc_ref, sh_ref, w_ref, cw_ref, o_ref, ext_ref, *, tn, conv_cols, taps):
    i = pl.program_id(1)
    last = pl.num_programs(1) - 1
    tm = x_ref.shape[1]
    pad = taps // 2
    n = w_ref.shape[1]
    sc, sh = sc_ref[0], sh_ref[0]
    h = _modulate(x_ref[0], sc, sh).astype(BF16)
    hp = _modulate(xp_ref[0, 0], sc, sh).astype(BF16)
    hn = _modulate(xn_ref[0, 0], sc, sh).astype(BF16)
    qk_cols = 2 * conv_cols // 3
    for j in range(conv_cols // tn):
        cols = slice(j * tn, (j + 1) * tn)
        w = w_ref[:, cols]
        ext_ref[0:SUBLANES, :] = jnp.where(i > 0, jnp.dot(hp, w, preferred_element_type=F32), 0.0)
        ext_ref[SUBLANES:SUBLANES + tm, :] = jnp.dot(h, w, preferred_element_type=F32)
        ext_ref[SUBLANES + tm:, :] = jnp.where(i < last, jnp.dot(hn, w, preferred_element_type=F32), 0.0)
        acc = jnp.zeros((tm, tn), F32)
        for t in range(taps):
            acc = acc + ext_ref[pl.ds(SUBLANES - pad + t, tm), :] * cw_ref[t:t + 1, cols]
        y = _silu(acc)
        for g in range(tn // LANES):
            yg = y[:, g * LANES:(g + 1) * LANES]
            c0 = j * tn + g * LANES
            if c0 < qk_cols:
                yg = yg * lax.rsqrt(jnp.sum(yg * yg, -1, keepdims=True) + RMS_EPS)
            if c0 < qk_cols // 2:
                yg = yg * (LANES ** -0.5)
            o_ref[0, :, c0:c0 + LANES] = yg
    for c0 in range(conv_cols, n, tn):
        c1 = min(c0 + tn, n)
        o_ref[0, :, c0:c1] = jnp.dot(h, w_ref[:, c0:c1], preferred_element_type=F32)


def _gdn_proj(x, scale, shift, w, conv_w, conv_cols):
    b, l, d = x.shape
    n = w.shape[1]
    tm = _row_tile(l, ROW_TILE)
    nblk = l // tm
    taps = conv_w.shape[0]
    tn = _col_tile(conv_cols)
    x4 = x.reshape(b, nblk, tm, d)
    zeros = jnp.zeros((b, 1, SUBLANES, d), x.dtype)
    xp = jnp.concatenate([zeros, x4[:, :-1, tm - SUBLANES:, :]], axis=1)
    xn = jnp.concatenate([x4[:, 1:, :SUBLANES, :], zeros], axis=1)
    conv_w = jnp.pad(conv_w, ((0, SUBLANES - taps), (0, 0)))
    vec = pl.BlockSpec((1, 1, d), lambda bi, i: (bi, 0, 0))
    halo = pl.BlockSpec((1, 1, SUBLANES, d), lambda bi, i: (bi, i, 0, 0))
    return pl.pallas_call(
        functools.partial(_gdn_proj_kernel, tn=tn, conv_cols=conv_cols, taps=taps),
        grid=(b, nblk),
        in_specs=[pl.BlockSpec((1, tm, d), lambda bi, i: (bi, i, 0)), halo, halo, vec, vec,
                  pl.BlockSpec((d, n), lambda bi, i: (0, 0)),
                  pl.BlockSpec((SUBLANES, conv_cols), lambda bi, i: (0, 0))],
        out_specs=pl.BlockSpec((1, tm, n), lambda bi, i: (bi, i, 0)),
        out_shape=jax.ShapeDtypeStruct((b, l, n), F32),
        scratch_shapes=[pltpu.VMEM((tm + 2 * SUBLANES, tn), F32)],
        compiler_params=_params("arbitrary", "arbitrary"),
        name="gdn_proj",
    )(x, xp, xn, scale, shift, w, conv_w)


def _gdn_kernel(q_ref, k_ref, v_ref, ab_ref, al_ref, dt_ref, s0_ref, o_ref, sf_ref, s_ref, qq_s, oc_s, ms_s, c0_s, gl_s,
                *, reverse, direction, heads):
    hg = pl.program_id(1)
    i = pl.program_id(2)
    tb = q_ref.shape[1]
    nc = tb // CHUNK
    per = s_ref.shape[0]

    @pl.when(i == 0)
    def _():
        s_ref[...] = s0_ref[0]
        qq_s[...] = jnp.zeros(qq_s.shape, F32)
        oc_s[...] = jnp.zeros(oc_s.shape, F32)
        ms_s[...] = jnp.zeros(ms_s.shape, F32)
        c0_s[...] = jnp.zeros(c0_s.shape, F32)
        gl_s[...] = jnp.ones(gl_s.shape, F32)

    state = [s_ref[hh] for hh in range(per)]
    pending = list(_chunk_order(nc, reverse))

    def tick():
        if pending:
            c = pending.pop(0)
            for hh in range(per):
                j = hh * nc + c
                o_ref[0, c * CHUNK:(c + 1) * CHUNK, hh * LANES:(hh + 1) * LANES] = _dot(qq_s[j], state[hh]) + oc_s[j]
                state[hh] = state[hh] * gl_s[j] - _dot(ms_s[j], state[hh]) + c0_s[j]

    def stack(x):
        return jnp.concatenate([_chunks(x[:, hh * LANES:(hh + 1) * LANES]) for hh in range(per)], axis=0)

    q, k, v = stack(q_ref[0]), stack(k_ref[0]), stack(v_ref[0])
    ab = ab_ref[0]
    lane = lax.broadcasted_iota(jnp.int32, (1, LANES), 1)
    betas, gs, gcs = [], [], []
    for hh in range(per):
        col = direction * 2 * heads + hg * per + hh
        betas.append(jax.nn.sigmoid(jnp.sum(jnp.where(lane == col, ab, 0.0), -1, keepdims=True)))
        gpre = jnp.sum(jnp.where(lane == col + heads, ab, 0.0), -1, keepdims=True)
        g_h = -jnp.exp(al_ref[hh]) * _softplus(gpre + dt_ref[hh])
        gs.append(_chunks(g_h))
        gcs.append(_chunks(_cumsum_chunks(g_h, reverse)))
    beta = jnp.concatenate(betas, axis=0).reshape(per * nc, CHUNK, 1)
    g_b = jnp.concatenate(gs, axis=0)
    gc_b = jnp.concatenate(gcs, axis=0)

    incl, strict, bd, eye = _masks(reverse, per * nc)
    gtot = jnp.sum(g_b, axis=1, keepdims=True)
    m = gc_b[:, :, :CHUNK] - jnp.swapaxes(gc_b, 1, 2)[:, :CHUNK, :]
    decay = jnp.exp(jnp.where(incl, m, -jnp.inf))
    kb = k * beta
    lm = jnp.where(strict, _bdot_nt(kb, k) * decay, 0.0)
    tick()
    t = _tri_inverse(lm, bd, eye, tick)
    eg = jnp.exp(gc_b)
    u0 = _bdot(t, v * beta)
    wk = _bdot(t, kb * eg)
    tick()
    qk = _bdot_nt(q, k) * decay
    kd = k * jnp.exp(gtot - gc_b)
    gl = jnp.exp(gtot)
    ms = _bdot_tn(kd, wk)
    c0 = _bdot_tn(kd, u0)
    qq = q * eg - _bdot(qk, wk)
    oc = _bdot(qk, u0)

    while pending:
        tick()
    for hh in range(per):
        s_ref[hh] = state[hh]
        sf_ref[0, hh] = state[hh]
    qq_s[...] = qq
    oc_s[...] = oc
    ms_s[...] = ms
    c0_s[...] = c0
    gl_s[...] = gl


def _gdn_core(p, a_log, dt_bias, s0, *, reverse, direction, heads):
    b, l, _ = p.shape
    tb = _row_tile(l, SCAN_TILE)
    nblk = l // tb
    per = GDN_HEADS_PER_STEP
    assert heads % per == 0
    width = per * LANES
    nb = per * (tb // CHUNK)
    cur, prev = _lagged_blocks(nblk, reverse)
    al = jnp.broadcast_to(a_log[direction][:, None, None], (heads, 1, LANES))
    dt = jnp.broadcast_to(dt_bias[direction][:, None, None], (heads, 1, LANES))

    def act(off):
        return pl.BlockSpec((1, tb, width), lambda bi, h, i: (bi, cur(i), off // per + h))

    per_head = pl.BlockSpec((per, 1, LANES), lambda bi, h, i: (h, 0, 0))
    state = pl.BlockSpec((1, per, LANES, LANES), lambda bi, h, i: (bi, h, 0, 0))
    ab_col = (3 * heads * LANES + heads * LANES) // LANES
    return pl.pallas_call(
        functools.partial(_gdn_kernel, reverse=reverse, direction=direction, heads=heads),
        grid=(b, heads // per, nblk + 1),
        in_specs=[act(0), act(heads), act(2 * heads),
                  pl.BlockSpec((1, tb, LANES), lambda bi, h, i: (bi, cur(i), ab_col)),
                  per_head, per_head, state],
        out_specs=[pl.BlockSpec((1, tb, width), lambda bi, h, i: (bi, prev(i), h)), state],
        out_shape=[jax.ShapeDtypeStruct((b, l, heads * LANES), F32),
                   jax.ShapeDtypeStruct(s0.shape, F32)],
        scratch_shapes=[pltpu.VMEM((per, LANES, LANES), F32),
                        pltpu.VMEM((nb, CHUNK, LANES), F32), pltpu.VMEM((nb, CHUNK, LANES), F32),
                        pltpu.VMEM((nb, LANES, LANES), F32), pltpu.VMEM((nb, LANES, LANES), F32),
                        pltpu.VMEM((nb, 1, LANES), F32)],
        compiler_params=_params("arbitrary", "arbitrary", "arbitrary"),
        name="gdn_bwd" if reverse else "gdn_fwd",
    )(p, p, p, p, al, dt, s0)


def _gla_kernel(q_ref, k_ref, v_ref, gr_ref, wg_ref, bg_ref, s0_ref, o_ref, sf_ref, s_ref, *, reverse):
    i = pl.program_id(2)
    tb = q_ref.shape[1]
    nc = tb // CHUNK

    @pl.when(i == 0)
    def _():
        s_ref[...] = s0_ref[0, 0]

    q = _chunks(q_ref[0] * (LANES ** -0.5))
    k = _chunks(k_ref[0])
    v = _chunks(v_ref[0])
    gr = gr_ref[0]
    gr_hi = gr.astype(BF16)
    gr_lo = (gr - gr_hi.astype(F32)).astype(BF16)
    gpre = (jnp.dot(gr_hi, wg_ref[0, 0], preferred_element_type=F32)
            + jnp.dot(gr_lo, wg_ref[0, 0], preferred_element_type=F32)
            + jnp.dot(gr_hi, wg_ref[0, 1], preferred_element_type=F32)) + bg_ref[0]
    g = -_softplus(-gpre) / GLA_TAU

    incl, _, _, _ = _masks(reverse, nc)
    bc = _chunks(_cumsum_chunks(g, reverse))
    btot = jnp.sum(_chunks(g), axis=1, keepdims=True)
    mid = CHUNK - 1 - CHUNK // 2 if reverse else CHUNK // 2
    ref = bc[:, mid:mid + 1, :]
    att = jnp.where(incl, _bdot_nt(q * jnp.exp(bc - ref), k * jnp.exp(ref - bc)), 0.0)
    o_intra = _bdot(att, v)
    qg = q * jnp.exp(bc)
    kv = _bdot_tn(v, k * jnp.exp(btot - bc))
    gl = jnp.exp(btot)

    st = s_ref[...]
    for c in _chunk_order(nc, reverse):
        o_ref[0, c * CHUNK:(c + 1) * CHUNK, :] = o_intra[c] + _dot_nt(qg[c], st)
        st = st * gl[c] + kv[c]
    s_ref[...] = st
    sf_ref[0, 0] = st


def _gla_core(p, wg, bg, s0, *, reverse, heads):
    b, l, _ = p.shape
    tb = _row_tile(l, SCAN_TILE)
    nblk = l // tb
    dk, dv = LANES, 2 * LANES
    blk = (lambda i: nblk - 1 - i) if reverse else (lambda i: i)
    gr_col = (2 * heads * dk + 2 * heads * dv) // LANES
    state = pl.BlockSpec((1, 1, dv, dk), lambda bi, h, i: (bi, h, 0, 0))
    return pl.pallas_call(
        functools.partial(_gla_kernel, reverse=reverse),
        grid=(b, heads, nblk),
        in_specs=[pl.BlockSpec((1, tb, dk), lambda bi, h, i: (bi, blk(i), h)),
                  pl.BlockSpec((1, tb, dk), lambda bi, h, i: (bi, blk(i), heads + h)),
                  pl.BlockSpec((1, tb, dv), lambda bi, h, i: (bi, blk(i), heads + h)),
                  pl.BlockSpec((1, tb, LANES), lambda bi, h, i: (bi, blk(i), gr_col)),
                  pl.BlockSpec((1, 2, LANES, dk), lambda bi, h, i: (h, 0, 0, 0)),
                  pl.BlockSpec((1, 1, dk), lambda bi, h, i: (h, 0, 0)),
                  state],
        out_specs=[pl.BlockSpec((1, tb, dv), lambda bi, h, i: (bi, blk(i), h)), state],
        out_shape=[jax.ShapeDtypeStruct((b, l, heads * dv), F32), jax.ShapeDtypeStruct(s0.shape, F32)],
        scratch_shapes=[pltpu.VMEM((dv, dk), F32)],
        compiler_params=_params("arbitrary", "arbitrary", "arbitrary"),
        name="gla_bwd" if reverse else "gla_fwd",
    )(p, p, p, p, wg, bg, s0)


def _rwkv_proj_kernel(x_ref, xp_ref, xn_ref, sc_ref, sh_ref, mu_ref, w4_ref, w1_ref, a1_ref, w2_ref, a2_ref,
                      w0_ref, a0_ref, r_ref, k_ref, v_ref, z_ref, lw_ref, a_ref):
    i = pl.program_id(1)
    last = pl.num_programs(1) - 1
    tm = x_ref.shape[1]
    rank = w2_ref.shape[1]
    sc, sh = sc_ref[0], sh_ref[0]
    u = _modulate(x_ref[0], sc, sh)
    up = jnp.where(i > 0, _modulate(xp_ref[0, 0, SUBLANES - 1:SUBLANES, :], sc, sh), 0.0)
    un = jnp.where(i < last, _modulate(xn_ref[0, 0, 0:1, :], sc, sh), 0.0)
    row = lax.broadcasted_iota(jnp.int32, (tm, 1), 0)
    prev = jnp.where(row == 0, up, pltpu.roll(u, 1, 0))
    nxt = jnp.where(row == tm - 1, un, pltpu.roll(u, tm - 1, 0))
    xx = 0.5 * (prev + nxt) - u

    def mix(j):
        return (u + xx * mu_ref[j:j + 1, :]).astype(BF16)

    r_ref[0] = jnp.dot(mix(0), w4_ref[0], preferred_element_type=F32)
    k_ref[0] = jnp.dot(mix(2), w4_ref[1], preferred_element_type=F32)
    v_ref[0] = jnp.dot(mix(3), w4_ref[2], preferred_element_type=F32)
    z_ref[0] = jnp.dot(mix(5), w4_ref[3], preferred_element_type=F32)
    lw1 = jnp.tanh(jnp.dot(mix(1), w1_ref[...], preferred_element_type=F32))
    a1 = jnp.dot(mix(4), a1_ref[...], preferred_element_type=F32)
    for n in range(2):
        wlog = w0_ref[n:n + 1, :] + _dot(lw1[:, n * rank:(n + 1) * rank], w2_ref[n])
        lw_ref[n, 0] = -jnp.exp(-_softplus(-wlog) - 0.5)
        a_ref[n, 0] = jax.nn.sigmoid(a0_ref[n:n + 1, :] + _dot(a1[:, n * rank:(n + 1) * rank], a2_ref[n]))


def _rwkv_proj(x, scale, shift, mu, w_rkvz, w0, w1, w2, a0, a1, a2):
    b, l, d = x.shape
    tm = _row_tile(l, 256)
    nblk = l // tm
    rank = w1.shape[2]
    x4 = x.reshape(b, nblk, tm, d)
    zeros = jnp.zeros((b, 1, SUBLANES, d), x.dtype)
    xp = jnp.concatenate([zeros, x4[:, :-1, tm - SUBLANES:, :]], axis=1)
    xn = jnp.concatenate([x4[:, 1:, :SUBLANES, :], zeros], axis=1)
    w1c = jnp.concatenate([w1[0], w1[1]], axis=1).astype(BF16)
    a1c = jnp.concatenate([a1[0], a1[1]], axis=1).astype(BF16)
    vec = pl.BlockSpec((1, 1, d), lambda bi, i: (bi, 0, 0))
    full = lambda a: pl.BlockSpec(a.shape, lambda bi, i: (0,) * a.ndim)
    halo = pl.BlockSpec((1, 1, SUBLANES, d), lambda bi, i: (bi, i, 0, 0))
    act = pl.BlockSpec((1, tm, d), lambda bi, i: (bi, i, 0))
    act2 = pl.BlockSpec((2, 1, tm, d), lambda bi, i: (0, bi, i, 0))
    w4 = w_rkvz.astype(BF16)
    w2b, a2b = w2.astype(BF16), a2.astype(BF16)
    sds = jax.ShapeDtypeStruct((b, l, d), F32)
    sds2 = jax.ShapeDtypeStruct((2, b, l, d), F32)
    return pl.pallas_call(
        _rwkv_proj_kernel,
        grid=(b, nblk),
        in_specs=[act, halo, halo, vec, vec, full(mu), full(w4), full(w1c), full(a1c), full(w2b), full(a2b),
                  full(w0), full(a0)],
        out_specs=[act, act, act, act, act2, act2],
        out_shape=[sds, sds, sds, sds, sds2, sds2],
        compiler_params=_params("arbitrary", "arbitrary"),
        name="rwkv_proj",
    )(x, xp, xn, scale, shift, mu, w4, w1c, a1c, w2b, a2b, w0, a0)


def _rwkv_kernel(r_ref, k_ref, v_ref, lw_ref, a_ref, kk_ref, ka_ref, rk_ref, s0_ref, y_ref, bv_ref, sf_ref, s_ref,
                 rq_s, yc_s, ms_s, c0_s, gl_s, *, reverse, n):
    i = pl.program_id(2)
    tb = r_ref.shape[1]
    nc = tb // CHUNK
    per = LANES // n

    @pl.when(i == 0)
    def _():
        s_ref[...] = s0_ref[0, 0]
        rq_s[...] = jnp.zeros(rq_s.shape, F32)
        yc_s[...] = jnp.zeros(yc_s.shape, F32)
        ms_s[...] = jnp.zeros(ms_s.shape, F32)
        c0_s[...] = jnp.zeros(c0_s.shape, F32)
        gl_s[...] = jnp.ones(gl_s.shape, F32)

    state = [s_ref[...]]
    pending = list(_chunk_order(nc, reverse))

    def tick():
        if pending:
            c = pending.pop(0)
            y_ref[0, c * CHUNK:(c + 1) * CHUNK, :] = _dot_nt(rq_s[c], state[0]) + yc_s[c]
            state[0] = state[0] * gl_s[c] - _dot(state[0], ms_s[c]) + c0_s[c]

    lane = lax.broadcasted_iota(jnp.int32, (1, LANES), 1)
    in_head = [lane // n == hh for hh in range(per)]

    def by_head(fn):
        out = fn(per - 1)
        for hh in range(per - 2, -1, -1):
            out = jnp.where(in_head[hh], fn(hh), out)
        return out

    def head_sum(x):
        return by_head(lambda hh: jnp.sum(jnp.where(in_head[hh], x, 0.0), -1, keepdims=True))

    r, kr, v, lw, a = r_ref[0], k_ref[0], v_ref[0], lw_ref[0, 0], a_ref[0, 0]
    kd = kr * (1.0 + (a - 1.0) * ka_ref[...])
    kq = kr * kk_ref[...]
    kk = kq * lax.rsqrt(head_sum(kq * kq) + RMS_EPS)
    kka = kk * a
    bv_ref[0] = head_sum(r * kd * rk_ref[...]) * v
    gc = _chunks(_cumsum_chunks(lw, reverse))
    r, kk, kka, kd, v, lw = _chunks(r), _chunks(kk), _chunks(kka), _chunks(kd), _chunks(v), _chunks(lw)
    gtot = jnp.sum(lw, axis=1, keepdims=True)
    gcx = gc - lw
    ref = 0.5 * gtot
    e_in = jnp.exp(ref - gc)
    e_out = jnp.exp(gtot - gc)
    gl = jnp.exp(gtot)
    at = kk * jnp.exp(gcx - ref)
    rt = r * jnp.exp(gc - ref)
    a0 = kk * jnp.exp(gcx)
    bh = kka * e_out

    def heads(x):
        return jnp.concatenate([x] * per, axis=0)

    def pick(x):
        return by_head(lambda hh: x[hh * nc:(hh + 1) * nc])

    lhs = jnp.concatenate([jnp.concatenate([jnp.where(in_head[hh], at, 0.0), jnp.where(in_head[hh], rt, 0.0)], axis=1)
                           for hh in range(per)], axis=0)
    rhs = jnp.concatenate([kka * e_in, kd * e_in], axis=1)
    row = lax.broadcasted_iota(jnp.int32, (1, 2 * CHUNK, 2 * CHUNK), 1)
    col = lax.broadcasted_iota(jnp.int32, (1, 2 * CHUNK, 2 * CHUNK), 2) & (CHUNK - 1)
    before = (col > row) if reverse else (col < row)
    upto = (col >= row - CHUNK) if reverse else (col <= row - CHUNK)
    keep = ((row < CHUNK) & before) | ((row >= CHUNK) & upto)
    aa = jnp.where(keep, _bdot_nt(lhs, heads(rhs)), 0.0)
    tick()
    top, bot = aa[:, :CHUNK, :], aa[:, CHUNK:, :]
    _, _, bd, eye = _masks(reverse, per * nc)
    t = _tri_inverse(top[:, :, :CHUNK], bd, eye, tick)
    zeros = jnp.zeros_like(v)
    akv = _bdot(top, heads(jnp.concatenate([zeros, v], axis=1)))
    tick()
    sol = _bdot(t, jnp.concatenate([akv, heads(a0)], axis=2))
    u0 = -pick(sol[:, :, :LANES])
    w = pick(sol[:, :, LANES:])
    ry = _bdot(bot, heads(jnp.concatenate([jnp.concatenate([w, zeros], axis=1),
                                           jnp.concatenate([u0, v], axis=1)], axis=2)))
    rq = r * jnp.exp(gc) - pick(ry[:, :, :LANES])
    yc = pick(ry[:, :, LANES:])
    r2 = lax.broadcasted_iota(jnp.int32, (1, LANES, LANES), 1)
    c2 = lax.broadcasted_iota(jnp.int32, (1, LANES, LANES), 2)
    same_head = (r2 // n) == (c2 // n)
    ms = jnp.where(same_head, _bdot_tn(w, bh), 0.0)
    c0 = jnp.where(same_head, _bdot_tn(jnp.concatenate([u0, v], axis=1),
                                       jnp.concatenate([bh, kd * e_out], axis=1)), 0.0)

    while pending:
        tick()
    s_ref[...] = state[0]
    sf_ref[0, 0] = state[0]
    rq_s[...] = rq
    yc_s[...] = yc
    ms_s[...] = ms
    c0_s[...] = c0
    gl_s[...] = gl


def _rwkv_core(r, k, v, lw, a, k_k, k_a, r_k, s0, *, reverse, direction):
    b, l, d = r.shape
    n = r_k.shape[-1]
    tb = _row_tile(l, SCAN_TILE)
    nblk = l // tb
    nc = tb // CHUNK
    cur, prev = _lagged_blocks(nblk, reverse)
    act = pl.BlockSpec((1, tb, LANES), lambda bi, h, i: (bi, cur(i), h))
    act2 = pl.BlockSpec((1, 1, tb, LANES), lambda bi, h, i: (direction, bi, cur(i), h))
    lagged = pl.BlockSpec((1, tb, LANES), lambda bi, h, i: (bi, prev(i), h))
    vec = pl.BlockSpec((1, LANES), lambda bi, h, i: (0, h))
    state = pl.BlockSpec((1, 1, LANES, LANES), lambda bi, h, i: (bi, h, 0, 0))
    sds = jax.ShapeDtypeStruct((b, l, d), F32)
    return pl.pallas_call(
        functools.partial(_rwkv_kernel, reverse=reverse, n=n),
        grid=(b, d // LANES, nblk + 1),
        in_specs=[act, act, act, act2, act2, vec, vec, vec, state],
        out_specs=[lagged, act, state],
        out_shape=[sds, sds, jax.ShapeDtypeStruct(s0.shape, F32)],
        scratch_shapes=[pltpu.VMEM((LANES, LANES), F32),
                        pltpu.VMEM((nc, CHUNK, LANES), F32), pltpu.VMEM((nc, CHUNK, LANES), F32),
                        pltpu.VMEM((nc, LANES, LANES), F32), pltpu.VMEM((nc, LANES, LANES), F32),
                        pltpu.VMEM((nc, 1, LANES), F32)],
        compiler_params=_params("arbitrary", "arbitrary", "arbitrary"),
        name="rwkv_bwd" if reverse else "rwkv_fwd",
    )(r, k, v, lw, a, k_k.reshape(1, d), k_a.reshape(1, d), r_k[direction].reshape(1, d), s0)


def _mla_proj_kernel(x_ref, sc_ref, sh_ref, win_ref, qn_ref, kvn_ref, wq_ref, wk_ref, wv_ref, cq_ref, s1_ref, s2_ref,
                     q_ref, k_ref, v_ref, z_ref, *, heads, q_lora, kv_lora, scale):
    d = z_ref.shape[2]
    h = _modulate(x_ref[0], sc_ref[0], sh_ref[0]).astype(BF16)
    p = jnp.dot(h, win_ref[...], preferred_element_type=F32)

    def rms(t, g):
        return t * lax.rsqrt(jnp.mean(t * t, -1, keepdims=True) + RMS_EPS) * g

    def rope(t):
        return (t * cq_ref[...] + pltpu.roll(t, LANES - LANES // 4, 1) * s1_ref[...]
                + pltpu.roll(t, LANES // 4, 1) * s2_ref[...])

    ql = rms(p[:, :q_lora], qn_ref[...]).astype(BF16)
    kvl = rms(p[:, q_lora:q_lora + kv_lora], kvn_ref[...]).astype(BF16)
    z_ref[0] = p[:, q_lora + kv_lora:q_lora + kv_lora + d]
    kr = rope(p[:, q_lora + kv_lora + d:])
    q = jnp.dot(ql, wq_ref[...], preferred_element_type=F32) * scale
    kn = jnp.dot(kvl, wk_ref[...], preferred_element_type=F32)
    v_ref[0] = jnp.dot(kvl, wv_ref[...], preferred_element_type=F32).astype(BF16)
    for j in range(heads):
        base = 2 * LANES * j
        q_ref[0, :, base:base + LANES] = q[:, base:base + LANES].astype(BF16)
        q_ref[0, :, base + LANES:base + 2 * LANES] = rope(q[:, base + LANES:base + 2 * LANES]).astype(BF16)
        k_ref[0, :, base:base + LANES] = kn[:, j * LANES:(j + 1) * LANES].astype(BF16)
        k_ref[0, :, base + LANES:base + 2 * LANES] = kr.astype(BF16)


def _mla_proj(x, scale, shift, w_in, q_norm, kv_norm, w_uq, w_ukv, cos, sin, *, heads, nope, rope, dv):
    b, l, d = x.shape
    q_lora, kv_lora = q_norm.shape[0], kv_norm.shape[0]
    tm = _row_tile(l, 256)
    half = rope // 2
    assert nope == LANES and dv == LANES and rope == LANES // 2
    o2 = q_lora + kv_lora
    w_in_r = jnp.concatenate([w_in[:, :o2], w_in[:, o2 + rope:], w_in[:, o2:o2 + rope],
                              jnp.zeros((d, LANES - rope), w_in.dtype)], axis=1).astype(BF16)
    wq = w_uq.reshape(q_lora, heads, nope + rope)
    wq = jnp.concatenate([wq, jnp.zeros((q_lora, heads, LANES - rope), w_uq.dtype)], axis=2)
    wq = wq.reshape(q_lora, heads * 2 * LANES).astype(BF16)
    wkv = w_ukv.reshape(kv_lora, heads, nope + dv)
    wk = wkv[:, :, :nope].reshape(kv_lora, heads * nope).astype(BF16)
    wv = wkv[:, :, nope:].reshape(kv_lora, heads * dv).astype(BF16)
    zer = jnp.zeros((l, half), F32)
    zer2 = jnp.zeros((l, LANES - rope), F32)
    cq = jnp.concatenate([cos, cos, zer2], axis=1)
    s1 = jnp.concatenate([-sin, zer, zer2], axis=1)
    s2 = jnp.concatenate([zer, sin, zer2], axis=1)
    vec = pl.BlockSpec((1, 1, d), lambda bi, i: (bi, 0, 0))
    full = lambda a: pl.BlockSpec(a.shape, lambda bi, i: (0,) * a.ndim)
    tab = pl.BlockSpec((tm, LANES), lambda bi, i: (i, 0))
    row = lambda w: pl.BlockSpec((1, tm, w), lambda bi, i: (bi, i, 0))
    qn = q_norm.reshape(1, q_lora)
    kvn = kv_norm.reshape(1, kv_lora)
    return pl.pallas_call(
        functools.partial(_mla_proj_kernel, heads=heads, q_lora=q_lora, kv_lora=kv_lora,
                          scale=(nope + rope) ** -0.5 * math.log2(math.e)),
        grid=(b, l // tm),
        in_specs=[row(d), vec, vec, full(w_in_r), full(qn), full(kvn), full(wq), full(wk), full(wv), tab, tab, tab],
        out_specs=[row(heads * 2 * LANES), row(heads * 2 * LANES), row(heads * dv), row(d)],
        out_shape=[jax.ShapeDtypeStruct((b, l, heads * 2 * LANES), BF16),
                   jax.ShapeDtypeStruct((b, l, heads * 2 * LANES), BF16),
                   jax.ShapeDtypeStruct((b, l, heads * dv), BF16),
                   jax.ShapeDtypeStruct((b, l, d), F32)],
        compiler_params=_params("arbitrary", "arbitrary"),
        name="mla_proj",
    )(x, scale, shift, w_in_r, qn, kvn, wq, wk, wv, cq, s1, s2)


def _flash_kernel(q_ref, k_ref, v_ref, o_ref, m_ref, acc_ref):
    j = pl.program_id(3)

    @pl.when(j == 0)
    def _():
        m_ref[...] = jnp.full(m_ref.shape, -jnp.inf, F32)
        acc_ref[...] = jnp.zeros(acc_ref.shape, F32)

    groups = q_ref.shape[1] // FLASH_ROWS

    def scores(g):
        return lax.dot_general(q_ref[0, g * FLASH_ROWS:(g + 1) * FLASH_ROWS, :], k_ref[0],
                               (((1,), (1,)), ((), ())), preferred_element_type=F32)

    v1 = jnp.concatenate([v_ref[0], jnp.ones(v_ref.shape[1:], BF16)], axis=1)
    s_next = scores(0)
    for g in range(groups):
        rows = slice(g * FLASH_ROWS, (g + 1) * FLASH_ROWS)
        s = s_next
        if g + 1 < groups:
            s_next = scores(g + 1)
        m_old = m_ref[rows, :]
        m_new = jnp.maximum(m_old, jnp.max(s, -1, keepdims=True))
        alpha = jnp.exp2(m_old - m_new)
        p = jnp.exp2(s - m_new)
        acc_ref[rows, :] = alpha * acc_ref[rows, :] + jnp.dot(p.astype(BF16), v1, preferred_element_type=F32)
        m_ref[rows, :] = m_new

    @pl.when(j == pl.num_programs(3) - 1)
    def _():
        o_ref[0] = acc_ref[:, :LANES] / acc_ref[:, LANES:]


def _flash(q, k, v, *, heads):
    b, l, _ = q.shape
    lk = k.shape[1]
    tq = _row_tile(l, FLASH_Q)
    tk = LANES
    for t in range(LANES, FLASH_K + 1, LANES):
        if lk % t == 0:
            tk = t
    return pl.pallas_call(
        _flash_kernel,
        grid=(b, heads, l // tq, lk // tk),
        in_specs=[pl.BlockSpec((1, tq, 2 * LANES), lambda bi, h, i, j: (bi, i, h)),
                  pl.BlockSpec((1, tk, 2 * LANES), lambda bi, h, i, j: (bi, j, h)),
                  pl.BlockSpec((1, tk, LANES), lambda bi, h, i, j: (bi, j, h))],
        out_specs=pl.BlockSpec((1, tq, LANES), lambda bi, h, i, j: (bi, i, h)),
        out_shape=jax.ShapeDtypeStruct((b, l, heads * LANES), F32),
        scratch_shapes=[pltpu.VMEM((tq, 1), F32), pltpu.VMEM((tq, 2 * LANES), F32)],
        compiler_params=_params("arbitrary", "arbitrary", "arbitrary", "arbitrary"),
        name="flash",
    )(q, k, v)


def _pad_cols(w):
    n = w.shape[1]
    return jnp.pad(w, ((0, 0), (0, -n % LANES))).astype(BF16)


def _gdn_layer(x, xc, mod, mod_c, w_in, conv_w, a_log, dt_bias, norm_g, w_out, ln_g, ln_b, alpha, need_ctx):
    heads = a_log.shape[1]
    dk = dv = norm_g.shape[0]
    assert dk == LANES and conv_w.shape[1] == 3 * heads * dk
    b = x.shape[0]
    w = _pad_cols(w_in)
    zero = jnp.zeros((b, heads, dk, dv), F32)

    def run(xs, m, s_f, s_b):
        p = _gdn_proj(xs, m[1], m[0], w, conv_w, 3 * heads * dk)
        core = functools.partial(_gdn_core, p, a_log, dt_bias, heads=heads)
        o_f, s_f = core(s_f, reverse=False, direction=0)
        o_b, s_b = core(s_b, reverse=True, direction=1)
        return p, o_f, o_b, s_f, s_b

    def finish(xs, m, p, o_f, o_b):
        specs = lambda tm: [_cols(tm, heads * dv, 0), _cols(tm, heads * dv, 0), _cols(tm, heads * dv, 3)]
        return _out("rms", [o_f, o_b, p], specs, [norm_g.reshape(1, dv)], xs, m[2], w_out.astype(BF16),
                    ln_g, ln_b, alpha, width=dv)

    pc, oc_f, oc_b, s_f, s_b = run(xc, mod_c, zero, zero)
    p, o_f, o_b, _, _ = run(x, mod, s_f, s_b)
    x_new = finish(x, mod, p, o_f, o_b)
    xc_new = finish(xc, mod_c, pc, oc_f, oc_b) if need_ctx else None
    return x_new, xc_new


def _gla_layer(x, xc, mod, mod_c, w_in, w_g2, b_g, norm_g, w_out, ln_g, ln_b, alpha, need_ctx):
    dv = norm_g.shape[0]
    rank, qk = w_g2.shape[1], w_g2.shape[2]
    dk = LANES
    heads = qk // dk
    assert dv == 2 * LANES and w_in.shape[1] == 2 * qk + 2 * heads * dv + 2 * rank
    b = x.shape[0]
    w = _pad_cols(w_in)
    zero = jnp.zeros((b, heads, dv, dk), F32)

    def gate_w(direction):
        wg = jnp.zeros((LANES, qk), F32).at[direction * rank:(direction + 1) * rank].set(w_g2[direction])
        wg = wg.reshape(LANES, heads, dk).transpose(1, 0, 2)
        hi = wg.astype(BF16)
        lo = (wg - hi.astype(F32)).astype(BF16)
        return jnp.stack([hi, lo], axis=1), b_g[direction].reshape(heads, 1, dk)

    def run(xs, m, s_f, s_b):
        p = _proj(xs, m[1], m[0], w)
        o_f, s_f = _gla_core(p, *gate_w(0), s_f, reverse=False, heads=heads)
        o_b, s_b = _gla_core(p, *gate_w(1), s_b, reverse=True, heads=heads)
        return p, o_f, o_b, s_f, s_b

    def finish(xs, m, p, o_f, o_b):
        specs = lambda tm: [_cols(tm, heads * dv, 0), _cols(tm, heads * dv, 0), _cols(tm, heads * dv, 2)]
        return _out("rms", [o_f, o_b, p], specs, [norm_g.reshape(1, dv)], xs, m[2], w_out.astype(BF16),
                    ln_g, ln_b, alpha, width=dv)

    pc, oc_f, oc_b, s_f, s_b = run(xc, mod_c, zero, zero)
    p, o_f, o_b, _, _ = run(x, mod, s_f, s_b)
    x_new = finish(x, mod, p, o_f, o_b)
    xc_new = finish(xc, mod_c, pc, oc_f, oc_b) if need_ctx else None
    return x_new, xc_new


def _rwkv_layer(x, xc, mod, mod_c, mu, w_rkvz, w0, w1, w2, a0, a1, a2, k_k, k_a, r_k, gn_g, gn_b, w_out,
                ln_g, ln_b, alpha, need_ctx):
    b, _, d = x.shape
    heads, n = r_k.shape
    zero = jnp.zeros((b, d // LANES, LANES, LANES), F32)
    r_k2 = jnp.broadcast_to(r_k[None], (2, heads, n))

    def run(xs, m, s_f, s_b):
        r, k, v, z, lw, a = _rwkv_proj(xs, m[1], m[0], mu, w_rkvz, w0, w1, w2, a0, a1, a2)
        core = functools.partial(_rwkv_core, r, k, v, lw, a, k_k, k_a, r_k2)
        y_f, bv_f, s_f = core(s_f, reverse=False, direction=0)
        y_b, bv_b, s_b = core(s_b, reverse=True, direction=1)
        return (y_f, y_b, bv_f, bv_b, z), s_f, s_b

    def finish(xs, m, acts):
        specs = lambda tm: [_cols(tm, d, 0)] * 5
        return _out("rwkv", list(acts), specs, [gn_g.reshape(1, d), gn_b.reshape(1, d)], xs, m[2],
                    w_out.astype(BF16), ln_g, ln_b, alpha, width=n)

    acts_c, s_f, s_b = run(xc, mod_c, zero, zero)
    acts, _, _ = run(x, mod, s_f, s_b)
    x_new = finish(x, mod, acts)
    xc_new = finish(xc, mod_c, acts_c) if need_ctx else None
    return x_new, xc_new


def _rope_tables(n_tokens, rope):
    rows = n_tokens // GRID_W
    row = jnp.repeat(jnp.arange(rows, dtype=F32), GRID_W)
    col = jnp.tile(jnp.arange(GRID_W, dtype=F32), rows)
    n_freq = rope // 4
    inv_freq = ROPE_BASE ** (-jnp.arange(n_freq, dtype=F32) / n_freq)
    ang = jnp.concatenate([row[:, None] * inv_freq, col[:, None] * inv_freq], axis=-1)
    return jnp.cos(ang), jnp.sin(ang)


def _mla_layer(x, xc, mod, mod_c, w_in, q_norm, kv_norm, w_uq, w_ukv, w_out, ln_g, ln_b, alpha):
    b, l, d = x.shape
    lc = xc.shape[1]
    q_lora, kv_lora = q_norm.shape[0], kv_norm.shape[0]
    dv = LANES
    heads = w_out.shape[0] // dv
    rope = w_in.shape[1] - q_lora - kv_lora - heads * dv
    nope = w_uq.shape[1] // heads - rope
    cos, sin = _rope_tables(l, rope)
    proj = functools.partial(_mla_proj, w_in=w_in, q_norm=q_norm, kv_norm=kv_norm, w_uq=w_uq, w_ukv=w_ukv,
                             heads=heads, nope=nope, rope=rope, dv=dv)
    q, k, v, z = proj(x, mod[1], mod[0], cos=cos, sin=sin)
    ones, zeros = jnp.ones((lc, rope // 2), F32), jnp.zeros((lc, rope // 2), F32)
    _, kc, vc, _ = proj(xc, mod_c[1], mod_c[0], cos=ones, sin=zeros)
    o = _flash(q, jnp.concatenate([k, kc], axis=1), jnp.concatenate([v, vc], axis=1), heads=heads)
    specs = lambda tm: [_cols(tm, d, 0), _cols(tm, d, 0)]
    return _out("mla", [o, z], specs, [], x, mod[2], w_out.astype(BF16), ln_g, ln_b, alpha)


def kernel(x, c, ctx, c_ctx, ada_w, ada_b, ln_g, ln_b, gdn_w_in, gdn_conv, gdn_a_log, gdn_dt_bias, gdn_norm, gdn_w_out, rwkv_mu, rwkv_w_rkvz, rwkv_w0, rwkv_w1, rwkv_w2, rwkv_a0, rwkv_a1, rwkv_a2, rwkv_k_k, rwkv_k_a, rwkv_r_k, rwkv_gn_g, rwkv_gn_b, rwkv_w_out, gla_w_in, gla_w_g2, gla_b_g, gla_norm, gla_w_out, mla_w_in, mla_q_norm, mla_kv_norm, mla_w_uq, mla_w_ukv, mla_w_out):
    b, _, d = x.shape
    depth = ada_w.shape[0]
    n_mixers = 4
    assert depth == n_mixers, "one layer of each mixer; the last (MLA) layer needs no context output"
    alpha = (2.0 * depth) ** 0.25

    rows = -(-(b + 1) // SUBLANES) * SUBLANES
    cvec = jnp.concatenate([c, c_ctx[None], jnp.zeros((rows - b - 1, d), F32)], axis=0)
    mods = _ada(cvec, ada_w, ada_b)

    def split(i):
        lat = [mods[i, :b, j * d:(j + 1) * d].reshape(b, 1, d) for j in range(3)]
        con = [jnp.broadcast_to(mods[i, b, j * d:(j + 1) * d].reshape(1, 1, d), (b, 1, d)) for j in range(3)]
        return lat, con

    xc = ctx
    m, mc = split(0)
    x, xc = _gdn_layer(x, xc, m, mc, gdn_w_in[0], gdn_conv[0], gdn_a_log[0], gdn_dt_bias[0], gdn_norm[0],
                       gdn_w_out[0], ln_g[0], ln_b[0], alpha, True)
    m, mc = split(1)
    x, xc = _rwkv_layer(x, xc, m, mc, rwkv_mu[0], rwkv_w_rkvz[0], rwkv_w0[0], rwkv_w1[0], rwkv_w2[0], rwkv_a0[0],
                        rwkv_a1[0], rwkv_a2[0], rwkv_k_k[0], rwkv_k_a[0], rwkv_r_k[0], rwkv_gn_g[0], rwkv_gn_b[0],
                        rwkv_w_out[0], ln_g[1], ln_b[1], alpha, True)
    m, mc = split(2)
    x, xc = _gla_layer(x, xc, m, mc, gla_w_in[0], gla_w_g2[0], gla_b_g[0], gla_norm[0], gla_w_out[0],
                       ln_g[2], ln_b[2], alpha, True)
    m, mc = split(3)
    return _mla_layer(x, xc, m, mc, mla_w_in[0], mla_q_norm[0], mla_kv_norm[0], mla_w_uq[0], mla_w_ukv[0],
                      mla_w_out[0], ln_g[3], ln_b[3], alpha)
```

```python
import functools
import math

import jax
import jax.numpy as jnp
from jax import lax
from jax.experimental import pallas as pl
from jax.experimental.pallas import tpu as pltpu

F32 = jnp.float32
BF16 = jnp.bfloat16
HI = lax.Precision.HIGHEST

LANES = 128
SUBLANES = 8
VMEM_LIMIT = 56 * 1024 * 1024

CHUNK = 64
SUB = 16
LN_EPS = 1e-5
RMS_EPS = 1e-6
GN_EPS = 64e-5
GLA_TAU = 16.0
ROPE_BASE = 10000.0
GRID_W = 64
ROW_TILE = 512
SCAN_TILE = 512
FLASH_Q = 2048
FLASH_K = 3328
FLASH_ROWS = 512
GLA_HEADS_PER_STEP = 4
GDN_HEADS_PER_STEP = 4


def _dot_hi(a, b):
    return jnp.dot(a, b, preferred_element_type=F32, precision=HI)


def _dot(a, b):
    return jnp.dot(a.astype(BF16), b.astype(BF16), preferred_element_type=F32)


def _bdot(a, b):
    return lax.dot_general(a.astype(BF16), b.astype(BF16), (((2,), (1,)), ((0,), (0,))),
                           preferred_element_type=F32)


def _bdot_tn(a, b):
    return lax.dot_general(a.astype(BF16), b.astype(BF16), (((1,), (1,)), ((0,), (0,))),
                           preferred_element_type=F32)


def _bdot_nt(a, b):
    return lax.dot_general(a.astype(BF16), b.astype(BF16), (((2,), (2,)), ((0,), (0,))),
                           preferred_element_type=F32)


def _dot_nt(a, b):
    return lax.dot_general(a.astype(BF16), b.astype(BF16), (((1,), (1,)), ((), ())),
                           preferred_element_type=F32)


def _silu(x):
    return x * jax.nn.sigmoid(x)


def _softplus(x):
    return jnp.maximum(x, 0.0) + jnp.log(1.0 + jnp.exp(-jnp.abs(x)))


def _params(*sem):
    return pltpu.CompilerParams(dimension_semantics=sem, vmem_limit_bytes=VMEM_LIMIT)


def _row_tile(n, cap):
    t = min(n, cap)
    assert n % t == 0 and t % SUBLANES == 0, (n, t)
    return t


def _col_tile(n, cap=768):
    assert n % LANES == 0, n
    best = LANES
    for t in range(LANES, cap + 1, LANES):
        if n % t == 0:
            best = t
    return best


def _masks(reverse, nc):
    r = lax.broadcasted_iota(jnp.int32, (nc, CHUNK, CHUNK), 1)
    c = lax.broadcasted_iota(jnp.int32, (nc, CHUNK, CHUNK), 2)
    if reverse:
        return c >= r, c > r
    return c <= r, c < r


def _tri_solve(lm, rhs, tick=lambda: None):
    r = lax.broadcasted_iota(jnp.int32, (1, CHUNK, LANES), 1)
    c = lax.broadcasted_iota(jnp.int32, (1, CHUNK, LANES), 2)
    left = c < CHUNK
    cl = c & (CHUNK - 1)
    eye_r = jnp.where((cl == r) & (c >= CHUNK), 1.0, 0.0)
    diag_blk = left & ((r // SUB) == (cl // SUB))
    d = jnp.where(diag_blk, lm, 0.0)
    e = jnp.where(left & ~diag_blk, lm, 0.0)

    def right(x):
        return jnp.where(left, 0.0, x)

    y = _bdot(d[:, :, :CHUNK], d - eye_r) + eye_r
    tick()
    for _ in range(3):
        y = _bdot(y[:, :, :CHUNK], y) + right(y)
        tick()
    t16 = right(y)
    zeros = jnp.zeros_like(e)
    f = _bdot(t16, jnp.concatenate([zeros, e], axis=1))
    tick()
    w = _bdot(f[:, :, :CHUNK], f + t16)
    z = jnp.where(left, w, t16 - w)
    tick()
    t = right(z + _bdot(z[:, :, :CHUNK], z))
    return _bdot(t, jnp.concatenate([jnp.zeros_like(rhs), rhs], axis=1))


def _chunks(x):
    return x.reshape(x.shape[0] // CHUNK, CHUNK, x.shape[1])


def _cumsum_chunks(x, reverse):
    rows = x.shape[0]
    pos = lax.broadcasted_iota(jnp.int32, (rows, 1), 0) & (CHUNK - 1)
    k = 1
    while k < CHUNK:
        if reverse:
            x = x + jnp.where(pos < CHUNK - k, pltpu.roll(x, rows - k, 0), 0.0)
        else:
            x = x + jnp.where(pos >= k, pltpu.roll(x, k, 0), 0.0)
        k *= 2
    return x


def _chunk_order(nc, reverse):
    return range(nc - 1, -1, -1) if reverse else range(nc)


def _lagged_blocks(nblk, reverse):
    order = (lambda j: nblk - 1 - j) if reverse else (lambda j: j)
    return (lambda i: order(jnp.minimum(i, nblk - 1))), (lambda i: order(jnp.maximum(i - 1, 0)))


def _modulate(x, scale, shift):
    return x * (1.0 + scale) + shift


def _ada_kernel(c_ref, w_ref, b_ref, o_ref):
    o_ref[0] = _dot_hi(_silu(c_ref[...]), w_ref[0]) + b_ref[0]


def _ada(cvec, ada_w, ada_b):
    depth, d, n = ada_w.shape
    tn = _col_tile(n, 512)
    rows = cvec.shape[0]
    return pl.pallas_call(
        _ada_kernel,
        grid=(depth, n // tn),
        in_specs=[pl.BlockSpec((rows, d), lambda i, j: (0, 0)),
                  pl.BlockSpec((1, d, tn), lambda i, j: (i, 0, j)),
                  pl.BlockSpec((1, 1, tn), lambda i, j: (i, 0, j))],
        out_specs=pl.BlockSpec((1, rows, tn), lambda i, j: (i, 0, j)),
        out_shape=jax.ShapeDtypeStruct((depth, rows, n), F32),
        compiler_params=_params("arbitrary", "arbitrary"),
        name="ada",
    )(cvec, ada_w, ada_b.reshape(depth, 1, n))


def _proj_kernel(x_ref, sc_ref, sh_ref, w_ref, o_ref, *, tn):
    h = _modulate(x_ref[0], sc_ref[0], sh_ref[0]).astype(BF16)
    for j in range(w_ref.shape[1] // tn):
        o_ref[0, :, j * tn:(j + 1) * tn] = jnp.dot(h, w_ref[:, j * tn:(j + 1) * tn],
                                                   preferred_element_type=F32)


def _proj(x, scale, shift, w):
    b, l, d = x.shape
    n = w.shape[1]
    tm = _row_tile(l, ROW_TILE)
    vec = pl.BlockSpec((1, 1, d), lambda bi, i: (bi, 0, 0))
    return pl.pallas_call(
        functools.partial(_proj_kernel, tn=_col_tile(n)),
        grid=(b, l // tm),
        in_specs=[pl.BlockSpec((1, tm, d), lambda bi, i: (bi, i, 0)), vec, vec,
                  pl.BlockSpec((d, n), lambda bi, i: (0, 0))],
        out_specs=pl.BlockSpec((1, tm, n), lambda bi, i: (bi, i, 0)),
        out_shape=jax.ShapeDtypeStruct((b, l, n), F32),
        compiler_params=_params("arbitrary", "arbitrary"),
        name="proj",
    )(x, scale, shift, w)


def _group_rms(o, g, width):
    parts = []
    for j in range(o.shape[1] // width):
        oj = o[:, j * width:(j + 1) * width]
        parts.append(oj * lax.rsqrt(jnp.mean(oj * oj, -1, keepdims=True) + RMS_EPS) * g)
    return jnp.concatenate(parts, axis=1)


def _out_tail(pre, x_ref, gate_ref, w_ref, lg_ref, lb_ref, o_ref, alpha):
    y = jnp.dot(pre.astype(BF16), w_ref[...], preferred_element_type=F32)
    r = alpha * x_ref[0] + gate_ref[0] * y
    rc = r - jnp.mean(r, -1, keepdims=True)
    o_ref[0] = rc * lax.rsqrt(jnp.mean(rc * rc, -1, keepdims=True) + LN_EPS) * lg_ref[...] + lb_ref[...]


def _out_rms_kernel(of_ref, ob_ref, z_ref, g_ref, x_ref, gate_ref, w_ref, lg_ref, lb_ref, o_ref, *, alpha, width):
    o = _group_rms(of_ref[0] + ob_ref[0], g_ref[...], width)
    _out_tail(o * _silu(z_ref[0]), x_ref, gate_ref, w_ref, lg_ref, lb_ref, o_ref, alpha)


def _out_rwkv_kernel(yf_ref, yb_ref, bf_ref, bb_ref, z_ref, g_ref, gb_ref, x_ref, gate_ref, w_ref, lg_ref, lb_ref,
                     o_ref, *, alpha, width):
    y = yf_ref[0] + yb_ref[0]
    parts = []
    for j in range(y.shape[1] // width):
        yj = y[:, j * width:(j + 1) * width]
        yc = yj - jnp.mean(yj, -1, keepdims=True)
        parts.append(yc * lax.rsqrt(jnp.mean(yc * yc, -1, keepdims=True) + GN_EPS))
    yn = jnp.concatenate(parts, axis=1) * g_ref[...] + gb_ref[...]
    pre = (yn + bf_ref[0] + bb_ref[0]) * _silu(z_ref[0])
    _out_tail(pre, x_ref, gate_ref, w_ref, lg_ref, lb_ref, o_ref, alpha)


def _out_mla_kernel(o_in_ref, z_ref, x_ref, gate_ref, w_ref, lg_ref, lb_ref, o_ref, *, alpha):
    _out_tail(o_in_ref[0] * _silu(z_ref[0]), x_ref, gate_ref, w_ref, lg_ref, lb_ref, o_ref, alpha)


def _out(kind, acts, act_specs, vecs, x, gate, w_out, ln_g, ln_b, alpha, width=None):
    b, l, d = x.shape
    tm = _row_tile(l, ROW_TILE)
    k = w_out.shape[0]
    body = {"rms": functools.partial(_out_rms_kernel, alpha=alpha, width=width),
            "rwkv": functools.partial(_out_rwkv_kernel, alpha=alpha, width=width),
            "mla": functools.partial(_out_mla_kernel, alpha=alpha)}[kind]
    row = lambda n: pl.BlockSpec((1, n), lambda bi, i: (0, 0))
    in_specs = (list(act_specs(tm)) + [row(v.shape[1]) for v in vecs]
                + [pl.BlockSpec((1, tm, d), lambda bi, i: (bi, i, 0)),
                   pl.BlockSpec((1, 1, d), lambda bi, i: (bi, 0, 0)),
                   pl.BlockSpec((k, d), lambda bi, i: (0, 0)), row(d), row(d)])
    return pl.pallas_call(
        body,
        grid=(b, l // tm),
        in_specs=in_specs,
        out_specs=pl.BlockSpec((1, tm, d), lambda bi, i: (bi, i, 0)),
        out_shape=jax.ShapeDtypeStruct((b, l, d), F32),
        compiler_params=_params("arbitrary", "arbitrary"),
        name="out_" + kind,
    )(*acts, *vecs, x, gate, w_out, ln_g.reshape(1, d), ln_b.reshape(1, d))


def _cols(tm, width, col):
    return pl.BlockSpec((1, tm, width), lambda bi, i: (bi, i, col))


def _gdn_proj_kernel(x_ref, xp_ref, xn_ref, sc_ref, sh_ref, w_ref, cw_ref, o_ref, ext_ref, *, tn, conv_cols, taps):
    i = pl.program_id(1)
    last = pl.num_programs(1) - 1
    tm = x_ref.shape[1]
    pad = taps // 2
    n = w_ref.shape[1]
    sc, sh = sc_ref[0], sh_ref[0]
    h = _modulate(x_ref[0], sc, sh).astype(BF16)
    hp = _modulate(xp_ref[0, 0], sc, sh).astype(BF16)
    hn = _modulate(xn_ref[0, 0], sc, sh).astype(BF16)
    qk_cols = 2 * conv_cols // 3
    for j in range(conv_cols // tn):
        cols = slice(j * tn, (j + 1) * tn)
        w = w_ref[:, cols]
        ext_ref[0:SUBLANES, :] = jnp.where(i > 0, jnp.dot(hp, w, preferred_element_type=F32), 0.0)
        ext_ref[SUBLANES:SUBLANES + tm, :] = jnp.dot(h, w, preferred_element_type=F32)
        ext_ref[SUBLANES + tm:, :] = jnp.where(i < last, jnp.dot(hn, w, preferred_element_type=F32), 0.0)
        acc = jnp.zeros((tm, tn), F32)
        for t in range(taps):
            acc = acc + ext_ref[pl.ds(SUBLANES - pad + t, tm), :] * cw_ref[t:t + 1, cols]
        y = _silu(acc)
        for g in range(tn // LANES):
            yg = y[:, g * LANES:(g + 1) * LANES]
            c0 = j * tn + g * LANES
            if c0 < qk_cols:
                yg = yg * lax.rsqrt(jnp.sum(yg * yg, -1, keepdims=True) + RMS_EPS)
            if c0 < qk_cols // 2:
                yg = yg * (LANES ** -0.5)
            o_ref[0, :, c0:c0 + LANES] = yg
    for c0 in range(conv_cols, n, tn):
        c1 = min(c0 + tn, n)
        o_ref[0, :, c0:c1] = jnp.dot(h, w_ref[:, c0:c1], preferred_element_type=F32)


def _gdn_proj(x, scale, shift, w, conv_w, conv_cols):
    b, l, d = x.shape
    n = w.shape[1]
    tm = _row_tile(l, ROW_TILE)
    nblk = l // tm
    taps = conv_w.shape[0]
    tn = _col_tile(conv_cols)
    x4 = x.reshape(b, nblk, tm, d)
    zeros = jnp.zeros((b, 1, SUBLANES, d), x.dtype)
    xp = jnp.concatenate([zeros, x4[:, :-1, tm - SUBLANES:, :]], axis=1)
    xn = jnp.concatenate([x4[:, 1:, :SUBLANES, :], zeros], axis=1)
    conv_w = jnp.pad(conv_w, ((0, SUBLANES - taps), (0, 0)))
    vec = pl.BlockSpec((1, 1, d), lambda bi, i: (bi, 0, 0))
    halo = pl.BlockSpec((1, 1, SUBLANES, d), lambda bi, i: (bi, i, 0, 0))
    return pl.pallas_call(
        functools.partial(_gdn_proj_kernel, tn=tn, conv_cols=conv_cols, taps=taps),
        grid=(b, nblk),
        in_specs=[pl.BlockSpec((1, tm, d), lambda bi, i: (bi, i, 0)), halo, halo, vec, vec,
                  pl.BlockSpec((d, n), lambda bi, i: (0, 0)),
                  pl.BlockSpec((SUBLANES, conv_cols), lambda bi, i: (0, 0))],
        out_specs=pl.BlockSpec((1, tm, n), lambda bi, i: (bi, i, 0)),
        out_shape=jax.ShapeDtypeStruct((b, l, n), F32),
        scratch_shapes=[pltpu.VMEM((tm + 2 * SUBLANES, tn), F32)],
        compiler_params=_params("arbitrary", "arbitrary"),
        name="gdn_proj",
    )(x, xp, xn, scale, shift, w, conv_w)


def _gdn_kernel(q_ref, k_ref, v_ref, ab_ref, al_ref, dt_ref, s0_ref, o_ref, sf_ref, s_ref, qq_s, oc_s, ms_s, c0_s, gl_s,
                *, reverse, direction, heads):
    hg = pl.program_id(1)
    i = pl.program_id(2)
    tb = q_ref.shape[1]
    nc = tb // CHUNK
    per = s_ref.shape[0]

    @pl.when(i == 0)
    def _():
        s_ref[...] = s0_ref[0]
        qq_s[...] = jnp.zeros(qq_s.shape, F32)
        oc_s[...] = jnp.zeros(oc_s.shape, F32)
        ms_s[...] = jnp.zeros(ms_s.shape, F32)
        c0_s[...] = jnp.zeros(c0_s.shape, F32)
        gl_s[...] = jnp.ones(gl_s.shape, F32)

    state = [s_ref[hh] for hh in range(per)]
    pending = list(_chunk_order(nc, reverse))

    def tick():
        if pending:
            c = pending.pop(0)
            for hh in range(per):
                j = hh * nc + c
                o_ref[0, c * CHUNK:(c + 1) * CHUNK, hh * LANES:(hh + 1) * LANES] = _dot(qq_s[j], state[hh]) + oc_s[j]
                state[hh] = state[hh] * gl_s[j] - _dot(ms_s[j], state[hh]) + c0_s[j]

    def stack(x):
        return jnp.concatenate([_chunks(x[:, hh * LANES:(hh + 1) * LANES]) for hh in range(per)], axis=0)

    q, k, v = stack(q_ref[0]), stack(k_ref[0]), stack(v_ref[0])
    ab = ab_ref[0]
    lane = lax.broadcasted_iota(jnp.int32, (1, LANES), 1)
    betas, gs, gcs = [], [], []
    for hh in range(per):
        col = direction * 2 * heads + hg * per + hh
        betas.append(jax.nn.sigmoid(jnp.sum(jnp.where(lane == col, ab, 0.0), -1, keepdims=True)))
        gpre = jnp.sum(jnp.where(lane == col + heads, ab, 0.0), -1, keepdims=True)
        g_h = -jnp.exp(al_ref[hh]) * _softplus(gpre + dt_ref[hh])
        gs.append(_chunks(g_h))
        gcs.append(_chunks(_cumsum_chunks(g_h, reverse)))
    beta = jnp.concatenate(betas, axis=0).reshape(per * nc, CHUNK, 1)
    g_b = jnp.concatenate(gs, axis=0)
    gc_b = jnp.concatenate(gcs, axis=0)

    incl, strict = _masks(reverse, per * nc)
    gtot = jnp.sum(g_b, axis=1, keepdims=True)
    m = gc_b[:, :, :CHUNK] - jnp.swapaxes(gc_b, 1, 2)[:, :CHUNK, :]
    decay = jnp.exp(jnp.where(incl, m, -jnp.inf))
    kb = k * beta
    lm = jnp.where(strict, _bdot_nt(kb, k) * decay, 0.0)
    tick()
    eg = jnp.exp(gc_b)
    sol = _tri_solve(jnp.concatenate([lm, jnp.zeros_like(lm)], axis=2),
                     jnp.concatenate([v * beta, kb * eg], axis=2), tick)
    u0, wk = sol[:, :, :LANES], sol[:, :, LANES:]
    tick()
    qk = _bdot_nt(q, k) * decay
    kd = k * jnp.exp(gtot - gc_b)
    gl = jnp.exp(gtot)
    ms = _bdot_tn(kd, wk)
    c0 = _bdot_tn(kd, u0)
    qq = q * eg - _bdot(qk, wk)
    oc = _bdot(qk, u0)

    while pending:
        tick()
    for hh in range(per):
        s_ref[hh] = state[hh]
        sf_ref[0, hh] = state[hh]
    qq_s[...] = qq
    oc_s[...] = oc
    ms_s[...] = ms
    c0_s[...] = c0
    gl_s[...] = gl


def _gdn_core(p, a_log, dt_bias, s0, *, reverse, direction, heads):
    b, l, _ = p.shape
    tb = _row_tile(l, SCAN_TILE)
    nblk = l // tb
    per = GDN_HEADS_PER_STEP
    assert heads % per == 0
    width = per * LANES
    nb = per * (tb // CHUNK)
    cur, prev = _lagged_blocks(nblk, reverse)
    al = jnp.broadcast_to(a_log[direction][:, None, None], (heads, 1, LANES))
    dt = jnp.broadcast_to(dt_bias[direction][:, None, None], (heads, 1, LANES))

    def act(off):
        return pl.BlockSpec((1, tb, width), lambda bi, h, i: (bi, cur(i), off // per + h))

    per_head = pl.BlockSpec((per, 1, LANES), lambda bi, h, i: (h, 0, 0))
    state = pl.BlockSpec((1, per, LANES, LANES), lambda bi, h, i: (bi, h, 0, 0))
    ab_col = (3 * heads * LANES + heads * LANES) // LANES
    return pl.pallas_call(
        functools.partial(_gdn_kernel, reverse=reverse, direction=direction, heads=heads),
        grid=(b, heads // per, nblk + 1),
        in_specs=[act(0), act(heads), act(2 * heads),
                  pl.BlockSpec((1, tb, LANES), lambda bi, h, i: (bi, cur(i), ab_col)),
                  per_head, per_head, state],
        out_specs=[pl.BlockSpec((1, tb, width), lambda bi, h, i: (bi, prev(i), h)), state],
        out_shape=[jax.ShapeDtypeStruct((b, l, heads * LANES), F32),
                   jax.ShapeDtypeStruct(s0.shape, F32)],
        scratch_shapes=[pltpu.VMEM((per, LANES, LANES), F32),
                        pltpu.VMEM((nb, CHUNK, LANES), F32), pltpu.VMEM((nb, CHUNK, LANES), F32),
                        pltpu.VMEM((nb, LANES, LANES), F32), pltpu.VMEM((nb, LANES, LANES), F32),
                        pltpu.VMEM((nb, 1, LANES), F32)],
        compiler_params=_params("arbitrary", "arbitrary", "arbitrary"),
        name="gdn_bwd" if reverse else "gdn_fwd",
    )(p, p, p, p, al, dt, s0)


def _gla_kernel(q_ref, k_ref, v_ref, gr_ref, wg_ref, bg_ref, s0_ref, o_ref, sf_ref, s_ref, *, reverse):
    i = pl.program_id(2)
    tb = q_ref.shape[1]
    nc = tb // CHUNK
    per = s_ref.shape[0]
    dv = s_ref.shape[1]

    @pl.when(i == 0)
    def _():
        s_ref[...] = s0_ref[0]

    def stack(x, width):
        return jnp.concatenate([_chunks(x[:, hh * width:(hh + 1) * width]) for hh in range(per)], axis=0)

    q = stack(q_ref[0] * (LANES ** -0.5), LANES)
    k = stack(k_ref[0], LANES)
    v = stack(v_ref[0], dv)
    gr = gr_ref[0]
    gr_hi = gr.astype(BF16)
    gr_lo = (gr - gr_hi.astype(F32)).astype(BF16)
    gs, bcs = [], []
    for hh in range(per):
        gpre = (jnp.dot(gr_hi, wg_ref[hh, 0], preferred_element_type=F32)
                + jnp.dot(gr_lo, wg_ref[hh, 0], preferred_element_type=F32)
                + jnp.dot(gr_hi, wg_ref[hh, 1], preferred_element_type=F32)) + bg_ref[hh]
        g_h = -_softplus(-gpre) / GLA_TAU
        gs.append(_chunks(g_h))
        bcs.append(_chunks(_cumsum_chunks(g_h, reverse)))
    g = jnp.concatenate(gs, axis=0)
    bc = jnp.concatenate(bcs, axis=0)

    incl, _ = _masks(reverse, per * nc)
    btot = jnp.sum(g, axis=1, keepdims=True)
    mid = CHUNK - 1 - CHUNK // 2 if reverse else CHUNK // 2
    ref = bc[:, mid:mid + 1, :]
    att = jnp.where(incl, _bdot_nt(q * jnp.exp(bc - ref), k * jnp.exp(ref - bc)), 0.0)
    o_intra = _bdot(att, v)
    qg = q * jnp.exp(bc)
    kv = _bdot_tn(v, k * jnp.exp(btot - bc))
    gl = jnp.exp(btot)

    st = [s_ref[hh] for hh in range(per)]
    for c in _chunk_order(nc, reverse):
        for hh in range(per):
            j = hh * nc + c
            o_ref[0, c * CHUNK:(c + 1) * CHUNK, hh * dv:(hh + 1) * dv] = o_intra[j] + _dot_nt(qg[j], st[hh])
            st[hh] = st[hh] * gl[j] + kv[j]
    for hh in range(per):
        s_ref[hh] = st[hh]
        sf_ref[0, hh] = st[hh]


def _gla_core(p, wg, bg, s0, *, reverse, heads):
    b, l, _ = p.shape
    tb = _row_tile(l, SCAN_TILE)
    nblk = l // tb
    dk, dv = LANES, 2 * LANES
    per = GLA_HEADS_PER_STEP
    assert heads % per == 0
    groups = heads // per
    blk = (lambda i: nblk - 1 - i) if reverse else (lambda i: i)
    gr_col = (2 * heads * dk + 2 * heads * dv) // LANES
    state = pl.BlockSpec((1, per, dv, dk), lambda bi, h, i: (bi, h, 0, 0))
    return pl.pallas_call(
        functools.partial(_gla_kernel, reverse=reverse),
        grid=(b, groups, nblk),
        in_specs=[pl.BlockSpec((1, tb, per * dk), lambda bi, h, i: (bi, blk(i), h)),
                  pl.BlockSpec((1, tb, per * dk), lambda bi, h, i: (bi, blk(i), groups + h)),
                  pl.BlockSpec((1, tb, per * dv), lambda bi, h, i: (bi, blk(i), groups + h)),
                  pl.BlockSpec((1, tb, LANES), lambda bi, h, i: (bi, blk(i), gr_col)),
                  pl.BlockSpec((per, 2, LANES, dk), lambda bi, h, i: (h, 0, 0, 0)),
                  pl.BlockSpec((per, 1, dk), lambda bi, h, i: (h, 0, 0)),
                  state],
        out_specs=[pl.BlockSpec((1, tb, per * dv), lambda bi, h, i: (bi, blk(i), h)), state],
        out_shape=[jax.ShapeDtypeStruct((b, l, heads * dv), F32), jax.ShapeDtypeStruct(s0.shape, F32)],
        scratch_shapes=[pltpu.VMEM((per, dv, dk), F32)],
        compiler_params=_params("arbitrary", "arbitrary", "arbitrary"),
        name="gla_bwd" if reverse else "gla_fwd",
    )(p, p, p, p, wg, bg, s0)


def _rwkv_proj_kernel(x_ref, xp_ref, xn_ref, sc_ref, sh_ref, mu_ref, w4_ref, w1_ref, a1_ref, w2_ref, a2_ref,
                      w0_ref, a0_ref, r_ref, k_ref, v_ref, z_ref, lw_ref, a_ref):
    i = pl.program_id(1)
    last = pl.num_programs(1) - 1
    tm = x_ref.shape[1]
    rank = w2_ref.shape[1]
    sc, sh = sc_ref[0], sh_ref[0]
    u = _modulate(x_ref[0], sc, sh)
    up = jnp.where(i > 0, _modulate(xp_ref[0, 0, SUBLANES - 1:SUBLANES, :], sc, sh), 0.0)
    un = jnp.where(i < last, _modulate(xn_ref[0, 0, 0:1, :], sc, sh), 0.0)
    row = lax.broadcasted_iota(jnp.int32, (tm, 1), 0)
    prev = jnp.where(row == 0, up, pltpu.roll(u, 1, 0))
    nxt = jnp.where(row == tm - 1, un, pltpu.roll(u, tm - 1, 0))
    xx = 0.5 * (prev + nxt) - u

    def mix(j):
        return (u + xx * mu_ref[j:j + 1, :]).astype(BF16)

    r_ref[0] = jnp.dot(mix(0), w4_ref[0], preferred_element_type=F32)
    k_ref[0] = jnp.dot(mix(2), w4_ref[1], preferred_element_type=F32)
    v_ref[0] = jnp.dot(mix(3), w4_ref[2], preferred_element_type=F32)
    z_ref[0] = jnp.dot(mix(5), w4_ref[3], preferred_element_type=F32)
    lw1 = jnp.tanh(jnp.dot(mix(1), w1_ref[...], preferred_element_type=F32))
    a1 = jnp.dot(mix(4), a1_ref[...], preferred_element_type=F32)
    for n in range(2):
        wlog = w0_ref[n:n + 1, :] + _dot(lw1[:, n * rank:(n + 1) * rank], w2_ref[n])
        lw_ref[n, 0] = -jnp.exp(-_softplus(-wlog) - 0.5)
        a_ref[n, 0] = jax.nn.sigmoid(a0_ref[n:n + 1, :] + _dot(a1[:, n * rank:(n + 1) * rank], a2_ref[n]))


def _rwkv_proj(x, scale, shift, mu, w_rkvz, w0, w1, w2, a0, a1, a2):
    b, l, d = x.shape
    tm = _row_tile(l, 256)
    nblk = l // tm
    rank = w1.shape[2]
    x4 = x.reshape(b, nblk, tm, d)
    zeros = jnp.zeros((b, 1, SUBLANES, d), x.dtype)
    xp = jnp.concatenate([zeros, x4[:, :-1, tm - SUBLANES:, :]], axis=1)
    xn = jnp.concatenate([x4[:, 1:, :SUBLANES, :], zeros], axis=1)
    w1c = jnp.concatenate([w1[0], w1[1]], axis=1).astype(BF16)
    a1c = jnp.concatenate([a1[0], a1[1]], axis=1).astype(BF16)
    vec = pl.BlockSpec((1, 1, d), lambda bi, i: (bi, 0, 0))
    full = lambda a: pl.BlockSpec(a.shape, lambda bi, i: (0,) * a.ndim)
    halo = pl.BlockSpec((1, 1, SUBLANES, d), lambda bi, i: (bi, i, 0, 0))
    act = pl.BlockSpec((1, tm, d), lambda bi, i: (bi, i, 0))
    act2 = pl.BlockSpec((2, 1, tm, d), lambda bi, i: (0, bi, i, 0))
    w4 = w_rkvz.astype(BF16)
    w2b, a2b = w2.astype(BF16), a2.astype(BF16)
    sds = jax.ShapeDtypeStruct((b, l, d), F32)
    sds2 = jax.ShapeDtypeStruct((2, b, l, d), F32)
    return pl.pallas_call(
        _rwkv_proj_kernel,
        grid=(b, nblk),
        in_specs=[act, halo, halo, vec, vec, full(mu), full(w4), full(w1c), full(a1c), full(w2b), full(a2b),
                  full(w0), full(a0)],
        out_specs=[act, act, act, act, act2, act2],
        out_shape=[sds, sds, sds, sds, sds2, sds2],
        compiler_params=_params("arbitrary", "arbitrary"),
        name="rwkv_proj",
    )(x, xp, xn, scale, shift, mu, w4, w1c, a1c, w2b, a2b, w0, a0)


def _rwkv_kernel(r_ref, k_ref, v_ref, lw_ref, a_ref, kk_ref, ka_ref, rk_ref, s0_ref, y_ref, bv_ref, sf_ref, s_ref,
                 rq_s, yc_s, ms_s, c0_s, gl_s, *, reverse, n):
    i = pl.program_id(2)
    tb = r_ref.shape[1]
    nc = tb // CHUNK
    per = LANES // n

    @pl.when(i == 0)
    def _():
        s_ref[...] = s0_ref[0, 0]
        rq_s[...] = jnp.zeros(rq_s.shape, F32)
        yc_s[...] = jnp.zeros(yc_s.shape, F32)
        ms_s[...] = jnp.zeros(ms_s.shape, F32)
        c0_s[...] = jnp.zeros(c0_s.shape, F32)
        gl_s[...] = jnp.ones(gl_s.shape, F32)

    state = [s_ref[...]]
    pending = list(_chunk_order(nc, reverse))

    def tick():
        if pending:
            c = pending.pop(0)
            y_ref[0, c * CHUNK:(c + 1) * CHUNK, :] = _dot_nt(rq_s[c], state[0]) + yc_s[c]
            state[0] = state[0] * gl_s[c] - _dot(state[0], ms_s[c]) + c0_s[c]

    lane = lax.broadcasted_iota(jnp.int32, (1, LANES), 1)
    in_head = [lane // n == hh for hh in range(per)]

    def by_head(fn):
        out = fn(per - 1)
        for hh in range(per - 2, -1, -1):
            out = jnp.where(in_head[hh], fn(hh), out)
        return out

    def head_sum(x):
        return by_head(lambda hh: jnp.sum(jnp.where(in_head[hh], x, 0.0), -1, keepdims=True))

    r, kr, v, lw, a = r_ref[0], k_ref[0], v_ref[0], lw_ref[0, 0], a_ref[0, 0]
    kd = kr * (1.0 + (a - 1.0) * ka_ref[...])
    kq = kr * kk_ref[...]
    kk = kq * lax.rsqrt(head_sum(kq * kq) + RMS_EPS)
    kka = kk * a
    bv_ref[0] = head_sum(r * kd * rk_ref[...]) * v
    gc = _chunks(_cumsum_chunks(lw, reverse))
    r, kk, kka, kd, v, lw = _chunks(r), _chunks(kk), _chunks(kka), _chunks(kd), _chunks(v), _chunks(lw)
    gtot = jnp.sum(lw, axis=1, keepdims=True)
    gcx = gc - lw
    ref = 0.5 * gtot
    e_in = jnp.exp(ref - gc)
    e_out = jnp.exp(gtot - gc)
    gl = jnp.exp(gtot)
    at = kk * jnp.exp(gcx - ref)
    rt = r * jnp.exp(gc - ref)
    a0 = kk * jnp.exp(gcx)
    bh = kka * e_out

    def heads(x):
        return jnp.concatenate([x] * per, axis=0)

    def pick(x):
        return by_head(lambda hh: x[hh * nc:(hh + 1) * nc])

    lhs = jnp.concatenate([jnp.concatenate([jnp.where(in_head[hh], at, 0.0), jnp.where(in_head[hh], rt, 0.0)], axis=1)
                           for hh in range(per)], axis=0)
    rhs = jnp.concatenate([kka * e_in, kd * e_in], axis=1)
    row = lax.broadcasted_iota(jnp.int32, (1, 2 * CHUNK, 2 * CHUNK), 1)
    col = lax.broadcasted_iota(jnp.int32, (1, 2 * CHUNK, 2 * CHUNK), 2) & (CHUNK - 1)
    before = (col > row) if reverse else (col < row)
    upto = (col >= row - CHUNK) if reverse else (col <= row - CHUNK)
    keep = ((row < CHUNK) & before) | ((row >= CHUNK) & upto)
    aa = jnp.where(keep, _bdot_nt(lhs, heads(rhs)), 0.0)
    tick()
    top, bot = aa[:, :CHUNK, :], aa[:, CHUNK:, :]
    zeros = jnp.zeros_like(v)
    akv = _bdot(top, heads(jnp.concatenate([zeros, v], axis=1)))
    tick()
    sol = _tri_solve(top, jnp.concatenate([akv, heads(a0)], axis=2), tick)
    u0 = -pick(sol[:, :, :LANES])
    w = pick(sol[:, :, LANES:])
    ry = _bdot(bot, heads(jnp.concatenate([jnp.concatenate([w, zeros], axis=1),
                                           jnp.concatenate([u0, v], axis=1)], axis=2)))
    rq = r * jnp.exp(gc) - pick(ry[:, :, :LANES])
    yc = pick(ry[:, :, LANES:])
    r2 = lax.broadcasted_iota(jnp.int32, (1, LANES, LANES), 1)
    c2 = lax.broadcasted_iota(jnp.int32, (1, LANES, LANES), 2)
    same_head = (r2 // n) == (c2 // n)
    ms = jnp.where(same_head, _bdot_tn(w, bh), 0.0)
    c0 = jnp.where(same_head, _bdot_tn(jnp.concatenate([u0, v], axis=1),
                                       jnp.concatenate([bh, kd * e_out], axis=1)), 0.0)

    while pending:
        tick()
    s_ref[...] = state[0]
    sf_ref[0, 0] = state[0]
    rq_s[...] = rq
    yc_s[...] = yc
    ms_s[...] = ms
    c0_s[...] = c0
    gl_s[...] = gl


def _rwkv_core(r, k, v, lw, a, k_k, k_a, r_k, s0, *, reverse, direction):
    b, l, d = r.shape
    n = r_k.shape[-1]
    tb = _row_tile(l, SCAN_TILE)
    nblk = l // tb
    nc = tb // CHUNK
    cur, prev = _lagged_blocks(nblk, reverse)
    act = pl.BlockSpec((1, tb, LANES), lambda bi, h, i: (bi, cur(i), h))
    act2 = pl.BlockSpec((1, 1, tb, LANES), lambda bi, h, i: (direction, bi, cur(i), h))
    lagged = pl.BlockSpec((1, tb, LANES), lambda bi, h, i: (bi, prev(i), h))
    vec = pl.BlockSpec((1, LANES), lambda bi, h, i: (0, h))
    state = pl.BlockSpec((1, 1, LANES, LANES), lambda bi, h, i: (bi, h, 0, 0))
    sds = jax.ShapeDtypeStruct((b, l, d), F32)
    return pl.pallas_call(
        functools.partial(_rwkv_kernel, reverse=reverse, n=n),
        grid=(b, d // LANES, nblk + 1),
        in_specs=[act, act, act, act2, act2, vec, vec, vec, state],
        out_specs=[lagged, act, state],
        out_shape=[sds, sds, jax.ShapeDtypeStruct(s0.shape, F32)],
        scratch_shapes=[pltpu.VMEM((LANES, LANES), F32),
                        pltpu.VMEM((nc, CHUNK, LANES), F32), pltpu.VMEM((nc, CHUNK, LANES), F32),
                        pltpu.VMEM((nc, LANES, LANES), F32), pltpu.VMEM((nc, LANES, LANES), F32),
                        pltpu.VMEM((nc, 1, LANES), F32)],
        compiler_params=_params("arbitrary", "arbitrary", "arbitrary"),
        name="rwkv_bwd" if reverse else "rwkv_fwd",
    )(r, k, v, lw, a, k_k.reshape(1, d), k_a.reshape(1, d), r_k[direction].reshape(1, d), s0)


def _mla_proj_kernel(x_ref, sc_ref, sh_ref, win_ref, qn_ref, kvn_ref, wq_ref, wk_ref, wv_ref, cq_ref, s1_ref, s2_ref,
                     q_ref, k_ref, v_ref, z_ref, *, heads, q_lora, kv_lora, scale):
    d = z_ref.shape[2]
    h = _modulate(x_ref[0], sc_ref[0], sh_ref[0]).astype(BF16)
    p = jnp.dot(h, win_ref[...], preferred_element_type=F32)

    def rms(t, g):
        return t * lax.rsqrt(jnp.mean(t * t, -1, keepdims=True) + RMS_EPS) * g

    def rope(t):
        return (t * cq_ref[...] + pltpu.roll(t, LANES - LANES // 4, 1) * s1_ref[...]
                + pltpu.roll(t, LANES // 4, 1) * s2_ref[...])

    ql = rms(p[:, :q_lora], qn_ref[...]).astype(BF16)
    kvl = rms(p[:, q_lora:q_lora + kv_lora], kvn_ref[...]).astype(BF16)
    z_ref[0] = p[:, q_lora + kv_lora:q_lora + kv_lora + d]
    kr = rope(p[:, q_lora + kv_lora + d:])
    q = jnp.dot(ql, wq_ref[...], preferred_element_type=F32) * scale
    kn = jnp.dot(kvl, wk_ref[...], preferred_element_type=F32)
    v_ref[0] = jnp.dot(kvl, wv_ref[...], preferred_element_type=F32).astype(BF16)
    for j in range(heads):
        base = 2 * LANES * j
        q_ref[0, :, base:base + LANES] = q[:, base:base + LANES].astype(BF16)
        q_ref[0, :, base + LANES:base + 2 * LANES] = rope(q[:, base + LANES:base + 2 * LANES]).astype(BF16)
        k_ref[0, :, base:base + LANES] = kn[:, j * LANES:(j + 1) * LANES].astype(BF16)
        k_ref[0, :, base + LANES:base + 2 * LANES] = kr.astype(BF16)


def _mla_proj(x, scale, shift, w_in, q_norm, kv_norm, w_uq, w_ukv, cos, sin, *, heads, nope, rope, dv):
    b, l, d = x.shape
    q_lora, kv_lora = q_norm.shape[0], kv_norm.shape[0]
    tm = _row_tile(l, 256)
    half = rope // 2
    assert nope == LANES and dv == LANES and rope == LANES // 2
    o2 = q_lora + kv_lora
    w_in_r = jnp.concatenate([w_in[:, :o2], w_in[:, o2 + rope:], w_in[:, o2:o2 + rope],
                              jnp.zeros((d, LANES - rope), w_in.dtype)], axis=1).astype(BF16)
    wq = w_uq.reshape(q_lora, heads, nope + rope)
    wq = jnp.concatenate([wq, jnp.zeros((q_lora, heads, LANES - rope), w_uq.dtype)], axis=2)
    wq = wq.reshape(q_lora, heads * 2 * LANES).astype(BF16)
    wkv = w_ukv.reshape(kv_lora, heads, nope + dv)
    wk = wkv[:, :, :nope].reshape(kv_lora, heads * nope).astype(BF16)
    wv = wkv[:, :, nope:].reshape(kv_lora, heads * dv).astype(BF16)
    zer = jnp.zeros((l, half), F32)
    zer2 = jnp.zeros((l, LANES - rope), F32)
    cq = jnp.concatenate([cos, cos, zer2], axis=1)
    s1 = jnp.concatenate([-sin, zer, zer2], axis=1)
    s2 = jnp.concatenate([zer, sin, zer2], axis=1)
    vec = pl.BlockSpec((1, 1, d), lambda bi, i: (bi, 0, 0))
    full = lambda a: pl.BlockSpec(a.shape, lambda bi, i: (0,) * a.ndim)
    tab = pl.BlockSpec((tm, LANES), lambda bi, i: (i, 0))
    row = lambda w: pl.BlockSpec((1, tm, w), lambda bi, i: (bi, i, 0))
    qn = q_norm.reshape(1, q_lora)
    kvn = kv_norm.reshape(1, kv_lora)
    return pl.pallas_call(
        functools.partial(_mla_proj_kernel, heads=heads, q_lora=q_lora, kv_lora=kv_lora,
                          scale=(nope + rope) ** -0.5 * math.log2(math.e)),
        grid=(b, l // tm),
        in_specs=[row(d), vec, vec, full(w_in_r), full(qn), full(kvn), full(wq), full(wk), full(wv), tab, tab, tab],
        out_specs=[row(heads * 2 * LANES), row(heads * 2 * LANES), row(heads * dv), row(d)],
        out_shape=[jax.ShapeDtypeStruct((b, l, heads * 2 * LANES), BF16),
                   jax.ShapeDtypeStruct((b, l, heads * 2 * LANES), BF16),
                   jax.ShapeDtypeStruct((b, l, heads * dv), BF16),
                   jax.ShapeDtypeStruct((b, l, d), F32)],
        compiler_params=_params("arbitrary", "arbitrary"),
        name="mla_proj",
    )(x, scale, shift, w_in_r, qn, kvn, wq, wk, wv, cq, s1, s2)


def _flash_kernel(q_ref, k_ref, v_ref, o_ref, m_ref, acc_ref):
    j = pl.program_id(3)

    @pl.when(j == 0)
    def _():
        m_ref[...] = jnp.full(m_ref.shape, -jnp.inf, F32)
        acc_ref[...] = jnp.zeros(acc_ref.shape, F32)

    groups = q_ref.shape[1] // FLASH_ROWS

    def scores(g):
        return lax.dot_general(q_ref[0, g * FLASH_ROWS:(g + 1) * FLASH_ROWS, :], k_ref[0],
                               (((1,), (1,)), ((), ())), preferred_element_type=F32)

    v1 = jnp.concatenate([v_ref[0], jnp.ones(v_ref.shape[1:], BF16)], axis=1)
    s_next = scores(0)
    for g in range(groups):
        rows = slice(g * FLASH_ROWS, (g + 1) * FLASH_ROWS)
        s = s_next
        if g + 1 < groups:
            s_next = scores(g + 1)
        m_old = m_ref[rows, :]
        m_new = jnp.maximum(m_old, jnp.max(s, -1, keepdims=True))
        alpha = jnp.exp2(m_old - m_new)
        p = jnp.exp2(s - m_new)
        acc_ref[rows, :] = alpha * acc_ref[rows, :] + jnp.dot(p.astype(BF16), v1, preferred_element_type=F32)
        m_ref[rows, :] = m_new

    @pl.when(j == pl.num_programs(3) - 1)
    def _():
        o_ref[0] = acc_ref[:, :LANES] / acc_ref[:, LANES:]


def _flash(q, k, v, *, heads):
    b, l, _ = q.shape
    lk = k.shape[1]
    tq = _row_tile(l, FLASH_Q)
    tk = LANES
    for t in range(LANES, FLASH_K + 1, LANES):
        if lk % t == 0:
            tk = t
    return pl.pallas_call(
        _flash_kernel,
        grid=(b, heads, l // tq, lk // tk),
        in_specs=[pl.BlockSpec((1, tq, 2 * LANES), lambda bi, h, i, j: (bi, i, h)),
                  pl.BlockSpec((1, tk, 2 * LANES), lambda bi, h, i, j: (bi, j, h)),
                  pl.BlockSpec((1, tk, LANES), lambda bi, h, i, j: (bi, j, h))],
        out_specs=pl.BlockSpec((1, tq, LANES), lambda bi, h, i, j: (bi, i, h)),
        out_shape=jax.ShapeDtypeStruct((b, l, heads * LANES), F32),
        scratch_shapes=[pltpu.VMEM((tq, 1), F32), pltpu.VMEM((tq, 2 * LANES), F32)],
        compiler_params=_params("arbitrary", "arbitrary", "arbitrary", "arbitrary"),
        name="flash",
    )(q, k, v)


def _pad_cols(w):
    n = w.shape[1]
    return jnp.pad(w, ((0, 0), (0, -n % LANES))).astype(BF16)


def _gdn_layer(x, xc, mod, mod_c, w_in, conv_w, a_log, dt_bias, norm_g, w_out, ln_g, ln_b, alpha, need_ctx):
    heads = a_log.shape[1]
    dk = dv = norm_g.shape[0]
    assert dk == LANES and conv_w.shape[1] == 3 * heads * dk
    b = x.shape[0]
    w = _pad_cols(w_in)
    zero = jnp.zeros((b, heads, dk, dv), F32)

    def run(xs, m, s_f, s_b):
        p = _gdn_proj(xs, m[1], m[0], w, conv_w, 3 * heads * dk)
        core = functools.partial(_gdn_core, p, a_log, dt_bias, heads=heads)
        o_f, s_f = core(s_f, reverse=False, direction=0)
        o_b, s_b = core(s_b, reverse=True, direction=1)
        return p, o_f, o_b, s_f, s_b

    def finish(xs, m, p, o_f, o_b):
        specs = lambda tm: [_cols(tm, heads * dv, 0), _cols(tm, heads * dv, 0), _cols(tm, heads * dv, 3)]
        return _out("rms", [o_f, o_b, p], specs, [norm_g.reshape(1, dv)], xs, m[2], w_out.astype(BF16),
                    ln_g, ln_b, alpha, width=dv)

    pc, oc_f, oc_b, s_f, s_b = run(xc, mod_c, zero, zero)
    p, o_f, o_b, _, _ = run(x, mod, s_f, s_b)
    x_new = finish(x, mod, p, o_f, o_b)
    xc_new = finish(xc, mod_c, pc, oc_f, oc_b) if need_ctx else None
    return x_new, xc_new


def _gla_layer(x, xc, mod, mod_c, w_in, w_g2, b_g, norm_g, w_out, ln_g, ln_b, alpha, need_ctx):
    dv = norm_g.shape[0]
    rank, qk = w_g2.shape[1], w_g2.shape[2]
    dk = LANES
    heads = qk // dk
    assert dv == 2 * LANES and w_in.shape[1] == 2 * qk + 2 * heads * dv + 2 * rank
    b = x.shape[0]
    w = _pad_cols(w_in)
    zero = jnp.zeros((b, heads, dv, dk), F32)

    def gate_w(direction):
        wg = jnp.zeros((LANES, qk), F32).at[direction * rank:(direction + 1) * rank].set(w_g2[direction])
        wg = wg.reshape(LANES, heads, dk).transpose(1, 0, 2)
        hi = wg.astype(BF16)
        lo = (wg - hi.astype(F32)).astype(BF16)
        return jnp.stack([hi, lo], axis=1), b_g[direction].reshape(heads, 1, dk)

    def run(xs, m, s_f, s_b):
        p = _proj(xs, m[1], m[0], w)
        o_f, s_f = _gla_core(p, *gate_w(0), s_f, reverse=False, heads=heads)
        o_b, s_b = _gla_core(p, *gate_w(1), s_b, reverse=True, heads=heads)
        return p, o_f, o_b, s_f, s_b

    def finish(xs, m, p, o_f, o_b):
        specs = lambda tm: [_cols(tm, heads * dv, 0), _cols(tm, heads * dv, 0), _cols(tm, heads * dv, 2)]
        return _out("rms", [o_f, o_b, p], specs, [norm_g.reshape(1, dv)], xs, m[2], w_out.astype(BF16),
                    ln_g, ln_b, alpha, width=dv)

    pc, oc_f, oc_b, s_f, s_b = run(xc, mod_c, zero, zero)
    p, o_f, o_b, _, _ = run(x, mod, s_f, s_b)
    x_new = finish(x, mod, p, o_f, o_b)
    xc_new = finish(xc, mod_c, pc, oc_f, oc_b) if need_ctx else None
    return x_new, xc_new


def _rwkv_layer(x, xc, mod, mod_c, mu, w_rkvz, w0, w1, w2, a0, a1, a2, k_k, k_a, r_k, gn_g, gn_b, w_out,
                ln_g, ln_b, alpha, need_ctx):
    b, _, d = x.shape
    heads, n = r_k.shape
    zero = jnp.zeros((b, d // LANES, LANES, LANES), F32)
    r_k2 = jnp.broadcast_to(r_k[None], (2, heads, n))

    def run(xs, m, s_f, s_b):
        r, k, v, z, lw, a = _rwkv_proj(xs, m[1], m[0], mu, w_rkvz, w0, w1, w2, a0, a1, a2)
        core = functools.partial(_rwkv_core, r, k, v, lw, a, k_k, k_a, r_k2)
        y_f, bv_f, s_f = core(s_f, reverse=False, direction=0)
        y_b, bv_b, s_b = core(s_b, reverse=True, direction=1)
        return (y_f, y_b, bv_f, bv_b, z), s_f, s_b

    def finish(xs, m, acts):
        specs = lambda tm: [_cols(tm, d, 0)] * 5
        return _out("rwkv", list(acts), specs, [gn_g.reshape(1, d), gn_b.reshape(1, d)], xs, m[2],
                    w_out.astype(BF16), ln_g, ln_b, alpha, width=n)

    acts_c, s_f, s_b = run(xc, mod_c, zero, zero)
    acts, _, _ = run(x, mod, s_f, s_b)
    x_new = finish(x, mod, acts)
    xc_new = finish(xc, mod_c, acts_c) if need_ctx else None
    return x_new, xc_new


def _rope_tables(n_tokens, rope):
    rows = n_tokens // GRID_W
    row = jnp.repeat(jnp.arange(rows, dtype=F32), GRID_W)
    col = jnp.tile(jnp.arange(GRID_W, dtype=F32), rows)
    n_freq = rope // 4
    inv_freq = ROPE_BASE ** (-jnp.arange(n_freq, dtype=F32) / n_freq)
    ang = jnp.concatenate([row[:, None] * inv_freq, col[:, None] * inv_freq], axis=-1)
    return jnp.cos(ang), jnp.sin(ang)


def _mla_layer(x, xc, mod, mod_c, w_in, q_norm, kv_norm, w_uq, w_ukv, w_out, ln_g, ln_b, alpha):
    b, l, d = x.shape
    lc = xc.shape[1]
    q_lora, kv_lora = q_norm.shape[0], kv_norm.shape[0]
    dv = LANES
    heads = w_out.shape[0] // dv
    rope = w_in.shape[1] - q_lora - kv_lora - heads * dv
    nope = w_uq.shape[1] // heads - rope
    cos, sin = _rope_tables(l, rope)
    proj = functools.partial(_mla_proj, w_in=w_in, q_norm=q_norm, kv_norm=kv_norm, w_uq=w_uq, w_ukv=w_ukv,
                             heads=heads, nope=nope, rope=rope, dv=dv)
    q, k, v, z = proj(x, mod[1], mod[0], cos=cos, sin=sin)
    ones, zeros = jnp.ones((lc, rope // 2), F32), jnp.zeros((lc, rope // 2), F32)
    _, kc, vc, _ = proj(xc, mod_c[1], mod_c[0], cos=ones, sin=zeros)
    o = _flash(q, jnp.concatenate([k, kc], axis=1), jnp.concatenate([v, vc], axis=1), heads=heads)
    specs = lambda tm: [_cols(tm, d, 0), _cols(tm, d, 0)]
    return _out("mla", [o, z], specs, [], x, mod[2], w_out.astype(BF16), ln_g, ln_b, alpha)


def kernel(x, c, ctx, c_ctx, ada_w, ada_b, ln_g, ln_b, gdn_w_in, gdn_conv, gdn_a_log, gdn_dt_bias, gdn_norm, gdn_w_out, rwkv_mu, rwkv_w_rkvz, rwkv_w0, rwkv_w1, rwkv_w2, rwkv_a0, rwkv_a1, rwkv_a2, rwkv_k_k, rwkv_k_a, rwkv_r_k, rwkv_gn_g, rwkv_gn_b, rwkv_w_out, gla_w_in, gla_w_g2, gla_b_g, gla_norm, gla_w_out, mla_w_in, mla_q_norm, mla_kv_norm, mla_w_uq, mla_w_ukv, mla_w_out):
    b, _, d = x.shape
    depth = ada_w.shape[0]
    n_mixers = 4
    assert depth == n_mixers, "one layer of each mixer; the last (MLA) layer needs no context output"
    alpha = (2.0 * depth) ** 0.25

    rows = -(-(b + 1) // SUBLANES) * SUBLANES
    cvec = jnp.concatenate([c, c_ctx[None], jnp.zeros((rows - b - 1, d), F32)], axis=0)
    mods = _ada(cvec, ada_w, ada_b)

    def split(i):
        lat = [mods[i, :b, j * d:(j + 1) * d].reshape(b, 1, d) for j in range(3)]
        con = [jnp.broadcast_to(mods[i, b, j * d:(j + 1) * d].reshape(1, 1, d), (b, 1, d)) for j in range(3)]
        return lat, con

    xc = ctx
    m, mc = split(0)
    x, xc = _gdn_layer(x, xc, m, mc, gdn_w_in[0], gdn_conv[0], gdn_a_log[0], gdn_dt_bias[0], gdn_norm[0],
                       gdn_w_out[0], ln_g[0], ln_b[0], alpha, True)
    m, mc = split(1)
    x, xc = _rwkv_layer(x, xc, m, mc, rwkv_mu[0], rwkv_w_rkvz[0], rwkv_w0[0], rwkv_w1[0], rwkv_w2[0], rwkv_a0[0],
                        rwkv_a1[0], rwkv_a2[0], rwkv_k_k[0], rwkv_k_a[0], rwkv_r_k[0], rwkv_gn_g[0], rwkv_gn_b[0],
                        rwkv_w_out[0], ln_g[1], ln_b[1], alpha, True)
    m, mc = split(2)
    x, xc = _gla_layer(x, xc, m, mc, gla_w_in[0], gla_w_g2[0], gla_b_g[0], gla_norm[0], gla_w_out[0],
                       ln_g[2], ln_b[2], alpha, True)
    m, mc = split(3)
    return _mla_layer(x, xc, m, mc, mla_w_in[0], mla_q_norm[0], mla_kv_norm[0], mla_w_uq[0], mla_w_ukv[0],
                      mla_w_out[0], ln_g[3], ln_b[3], alpha)
```

```python
import functools
import math

import jax
import jax.numpy as jnp
from jax import lax
from jax.experimental import pallas as pl
from jax.experimental.pallas import tpu as pltpu

F32 = jnp.float32
BF16 = jnp.bfloat16
HI = lax.Precision.HIGHEST

LANES = 128
SUBLANES = 8
VMEM_LIMIT = 56 * 1024 * 1024

CHUNK = 64
SUB = 16
LN_EPS = 1e-5
RMS_EPS = 1e-6
GN_EPS = 64e-5
GLA_TAU = 16.0
ROPE_BASE = 10000.0
GRID_W = 64
ROW_TILE = 512
SCAN_TILE = 512
FLASH_Q = 4096
FLASH_K = 3328
FLASH_ROWS = 512
GLA_HEADS_PER_STEP = 4
GDN_HEADS_PER_STEP = 4


def _dot_hi(a, b):
    return jnp.dot(a, b, preferred_element_type=F32, precision=HI)


def _dot(a, b):
    return jnp.dot(a.astype(BF16), b.astype(BF16), preferred_element_type=F32)


def _bdot(a, b):
    return lax.dot_general(a.astype(BF16), b.astype(BF16), (((2,), (1,)), ((0,), (0,))),
                           preferred_element_type=F32)


def _bdot_tn(a, b):
    return lax.dot_general(a.astype(BF16), b.astype(BF16), (((1,), (1,)), ((0,), (0,))),
                           preferred_element_type=F32)


def _bdot_nt(a, b):
    return lax.dot_general(a.astype(BF16), b.astype(BF16), (((2,), (2,)), ((0,), (0,))),
                           preferred_element_type=F32)


def _dot_nt(a, b):
    return lax.dot_general(a.astype(BF16), b.astype(BF16), (((1,), (1,)), ((), ())),
                           preferred_element_type=F32)


def _silu(x):
    return x * jax.nn.sigmoid(x)


def _softplus(x):
    return jnp.maximum(x, 0.0) + jnp.log(1.0 + jnp.exp(-jnp.abs(x)))


def _params(*sem):
    return pltpu.CompilerParams(dimension_semantics=sem, vmem_limit_bytes=VMEM_LIMIT)


def _row_tile(n, cap):
    t = min(n, cap)
    assert n % t == 0 and t % SUBLANES == 0, (n, t)
    return t


def _col_tile(n, cap=768):
    assert n % LANES == 0, n
    best = LANES
    for t in range(LANES, cap + 1, LANES):
        if n % t == 0:
            best = t
    return best


def _masks(reverse, nc):
    r = lax.broadcasted_iota(jnp.int32, (nc, CHUNK, CHUNK), 1)
    c = lax.broadcasted_iota(jnp.int32, (nc, CHUNK, CHUNK), 2)
    if reverse:
        return c >= r, c > r
    return c <= r, c < r


def _tri_solve(lm, rhs, tick=lambda: None):
    r = lax.broadcasted_iota(jnp.int32, (1, CHUNK, LANES), 1)
    c = lax.broadcasted_iota(jnp.int32, (1, CHUNK, LANES), 2)
    left = c < CHUNK
    cl = c & (CHUNK - 1)
    eye_r = jnp.where((cl == r) & (c >= CHUNK), 1.0, 0.0)
    diag_blk = left & ((r // SUB) == (cl // SUB))
    d = jnp.where(diag_blk, lm, 0.0)
    e = jnp.where(left & ~diag_blk, lm, 0.0)

    def right(x):
        return jnp.where(left, 0.0, x)

    y = _bdot(d[:, :, :CHUNK], d - eye_r) + eye_r
    tick()
    for _ in range(3):
        y = _bdot(y[:, :, :CHUNK], y) + right(y)
        tick()
    t16 = right(y)
    zeros = jnp.zeros_like(e)
    f = _bdot(t16, jnp.concatenate([zeros, e], axis=1))
    tick()
    w = _bdot(f[:, :, :CHUNK], f + t16)
    z = jnp.where(left, w, t16 - w)
    tick()
    t = right(z + _bdot(z[:, :, :CHUNK], z))
    return _bdot(t, jnp.concatenate([jnp.zeros_like(rhs), rhs], axis=1))


def _chunks(x):
    return x.reshape(x.shape[0] // CHUNK, CHUNK, x.shape[1])


def _cumsum_chunks(x, reverse):
    rows = x.shape[0]
    pos = lax.broadcasted_iota(jnp.int32, (rows, 1), 0) & (CHUNK - 1)
    k = 1
    while k < CHUNK:
        if reverse:
            x = x + jnp.where(pos < CHUNK - k, pltpu.roll(x, rows - k, 0), 0.0)
        else:
            x = x + jnp.where(pos >= k, pltpu.roll(x, k, 0), 0.0)
        k *= 2
    return x


def _chunk_order(nc, reverse):
    return range(nc - 1, -1, -1) if reverse else range(nc)


def _lagged_blocks(nblk, reverse):
    order = (lambda j: nblk - 1 - j) if reverse else (lambda j: j)
    return (lambda i: order(jnp.minimum(i, nblk - 1))), (lambda i: order(jnp.maximum(i - 1, 0)))


def _modulate(x, scale, shift):
    return x * (1.0 + scale) + shift


def _ada_kernel(c_ref, w_ref, b_ref, o_ref):
    o_ref[0] = _dot_hi(_silu(c_ref[...]), w_ref[0]) + b_ref[0]


def _ada(cvec, ada_w, ada_b):
    depth, d, n = ada_w.shape
    tn = _col_tile(n, 512)
    rows = cvec.shape[0]
    return pl.pallas_call(
        _ada_kernel,
        grid=(depth, n // tn),
        in_specs=[pl.BlockSpec((rows, d), lambda i, j: (0, 0)),
                  pl.BlockSpec((1, d, tn), lambda i, j: (i, 0, j)),
                  pl.BlockSpec((1, 1, tn), lambda i, j: (i, 0, j))],
        out_specs=pl.BlockSpec((1, rows, tn), lambda i, j: (i, 0, j)),
        out_shape=jax.ShapeDtypeStruct((depth, rows, n), F32),
        compiler_params=_params("arbitrary", "arbitrary"),
        name="ada",
    )(cvec, ada_w, ada_b.reshape(depth, 1, n))


def _proj_kernel(x_ref, sc_ref, sh_ref, w_ref, o_ref, *, tn):
    h = _modulate(x_ref[0], sc_ref[0], sh_ref[0]).astype(BF16)
    for j in range(w_ref.shape[1] // tn):
        o_ref[0, :, j * tn:(j + 1) * tn] = jnp.dot(h, w_ref[:, j * tn:(j + 1) * tn],
                                                   preferred_element_type=F32)


def _proj(x, scale, shift, w):
    b, l, d = x.shape
    n = w.shape[1]
    tm = _row_tile(l, ROW_TILE)
    vec = pl.BlockSpec((1, 1, d), lambda bi, i: (bi, 0, 0))
    return pl.pallas_call(
        functools.partial(_proj_kernel, tn=_col_tile(n)),
        grid=(b, l // tm),
        in_specs=[pl.BlockSpec((1, tm, d), lambda bi, i: (bi, i, 0)), vec, vec,
                  pl.BlockSpec((d, n), lambda bi, i: (0, 0))],
        out_specs=pl.BlockSpec((1, tm, n), lambda bi, i: (bi, i, 0)),
        out_shape=jax.ShapeDtypeStruct((b, l, n), F32),
        compiler_params=_params("arbitrary", "arbitrary"),
        name="proj",
    )(x, scale, shift, w)


def _group_rms(o, g, width):
    parts = []
    for j in range(o.shape[1] // width):
        oj = o[:, j * width:(j + 1) * width]
        parts.append(oj * lax.rsqrt(jnp.mean(oj * oj, -1, keepdims=True) + RMS_EPS) * g)
    return jnp.concatenate(parts, axis=1)


def _out_tail(pre, x_ref, gate_ref, w_ref, lg_ref, lb_ref, o_ref, alpha):
    y = jnp.dot(pre.astype(BF16), w_ref[...], preferred_element_type=F32)
    r = alpha * x_ref[0] + gate_ref[0] * y
    rc = r - jnp.mean(r, -1, keepdims=True)
    o_ref[0] = rc * lax.rsqrt(jnp.mean(rc * rc, -1, keepdims=True) + LN_EPS) * lg_ref[...] + lb_ref[...]


def _out_rms_kernel(of_ref, ob_ref, z_ref, g_ref, x_ref, gate_ref, w_ref, lg_ref, lb_ref, o_ref, *, alpha, width):
    o = _group_rms(of_ref[0] + ob_ref[0], g_ref[...], width)
    _out_tail(o * _silu(z_ref[0]), x_ref, gate_ref, w_ref, lg_ref, lb_ref, o_ref, alpha)


def _out_rwkv_kernel(yf_ref, yb_ref, bf_ref, bb_ref, z_ref, g_ref, gb_ref, x_ref, gate_ref, w_ref, lg_ref, lb_ref,
                     o_ref, *, alpha, width):
    y = yf_ref[0] + yb_ref[0]
    parts = []
    for j in range(y.shape[1] // width):
        yj = y[:, j * width:(j + 1) * width]
        yc = yj - jnp.mean(yj, -1, keepdims=True)
        parts.append(yc * lax.rsqrt(jnp.mean(yc * yc, -1, keepdims=True) + GN_EPS))
    yn = jnp.concatenate(parts, axis=1) * g_ref[...] + gb_ref[...]
    pre = (yn + bf_ref[0] + bb_ref[0]) * _silu(z_ref[0])
    _out_tail(pre, x_ref, gate_ref, w_ref, lg_ref, lb_ref, o_ref, alpha)


def _out_mla_kernel(o_in_ref, z_ref, x_ref, gate_ref, w_ref, lg_ref, lb_ref, o_ref, *, alpha):
    _out_tail(o_in_ref[0] * _silu(z_ref[0]), x_ref, gate_ref, w_ref, lg_ref, lb_ref, o_ref, alpha)


def _out(kind, acts, act_specs, vecs, x, gate, w_out, ln_g, ln_b, alpha, width=None):
    b, l, d = x.shape
    tm = _row_tile(l, ROW_TILE)
    k = w_out.shape[0]
    body = {"rms": functools.partial(_out_rms_kernel, alpha=alpha, width=width),
            "rwkv": functools.partial(_out_rwkv_kernel, alpha=alpha, width=width),
            "mla": functools.partial(_out_mla_kernel, alpha=alpha)}[kind]
    row = lambda n: pl.BlockSpec((1, n), lambda bi, i: (0, 0))
    in_specs = (list(act_specs(tm)) + [row(v.shape[1]) for v in vecs]
                + [pl.BlockSpec((1, tm, d), lambda bi, i: (bi, i, 0)),
                   pl.BlockSpec((1, 1, d), lambda bi, i: (bi, 0, 0)),
                   pl.BlockSpec((k, d), lambda bi, i: (0, 0)), row(d), row(d)])
    return pl.pallas_call(
        body,
        grid=(b, l // tm),
        in_specs=in_specs,
        out_specs=pl.BlockSpec((1, tm, d), lambda bi, i: (bi, i, 0)),
        out_shape=jax.ShapeDtypeStruct((b, l, d), F32),
        compiler_params=_params("arbitrary", "arbitrary"),
        name="out_" + kind,
    )(*acts, *vecs, x, gate, w_out, ln_g.reshape(1, d), ln_b.reshape(1, d))


def _cols(tm, width, col):
    return pl.BlockSpec((1, tm, width), lambda bi, i: (bi, i, col))


def _gdn_proj_kernel(x_ref, xp_ref, xn_ref, sc_ref, sh_ref, w_ref, cw_ref, o_ref, ext_ref, *, tn, conv_cols, taps):
    i = pl.program_id(1)
    last = pl.num_programs(1) - 1
    tm = x_ref.shape[1]
    pad = taps // 2
    n = w_ref.shape[1]
    sc, sh = sc_ref[0], sh_ref[0]
    h = _modulate(x_ref[0], sc, sh).astype(BF16)
    hp = _modulate(xp_ref[0, 0], sc, sh).astype(BF16)
    hn = _modulate(xn_ref[0, 0], sc, sh).astype(BF16)
    qk_cols = 2 * conv_cols // 3
    for j in range(conv_cols // tn):
        cols = slice(j * tn, (j + 1) * tn)
        w = w_ref[:, cols]
        ext_ref[0:SUBLANES, :] = jnp.where(i > 0, jnp.dot(hp, w, preferred_element_type=F32), 0.0)
        ext_ref[SUBLANES:SUBLANES + tm, :] = jnp.dot(h, w, preferred_element_type=F32)
        ext_ref[SUBLANES + tm:, :] = jnp.where(i < last, jnp.dot(hn, w, preferred_element_type=F32), 0.0)
        acc = jnp.zeros((tm, tn), F32)
        for t in range(taps):
            acc = acc + ext_ref[pl.ds(SUBLANES - pad + t, tm), :] * cw_ref[t:t + 1, cols]
        y = _silu(acc)
        for g in range(tn // LANES):
            yg = y[:, g * LANES:(g + 1) * LANES]
            c0 = j * tn + g * LANES
            if c0 < qk_cols:
                yg = yg * lax.rsqrt(jnp.sum(yg * yg, -1, keepdims=True) + RMS_EPS)
            if c0 < qk_cols // 2:
                yg = yg * (LANES ** -0.5)
            o_ref[0, :, c0:c0 + LANES] = yg
    for c0 in range(conv_cols, n, tn):
        c1 = min(c0 + tn, n)
        o_ref[0, :, c0:c1] = jnp.dot(h, w_ref[:, c0:c1], preferred_element_type=F32)


def _gdn_proj(x, scale, shift, w, conv_w, conv_cols):
    b, l, d = x.shape
    n = w.shape[1]
    tm = _row_tile(l, ROW_TILE)
    nblk = l // tm
    taps = conv_w.shape[0]
    tn = _col_tile(conv_cols)
    x4 = x.reshape(b, nblk, tm, d)
    zeros = jnp.zeros((b, 1, SUBLANES, d), x.dtype)
    xp = jnp.concatenate([zeros, x4[:, :-1, tm - SUBLANES:, :]], axis=1)
    xn = jnp.concatenate([x4[:, 1:, :SUBLANES, :], zeros], axis=1)
    conv_w = jnp.pad(conv_w, ((0, SUBLANES - taps), (0, 0)))
    vec = pl.BlockSpec((1, 1, d), lambda bi, i: (bi, 0, 0))
    halo = pl.BlockSpec((1, 1, SUBLANES, d), lambda bi, i: (bi, i, 0, 0))
    return pl.pallas_call(
        functools.partial(_gdn_proj_kernel, tn=tn, conv_cols=conv_cols, taps=taps),
        grid=(b, nblk),
        in_specs=[pl.BlockSpec((1, tm, d), lambda bi, i: (bi, i, 0)), halo, halo, vec, vec,
                  pl.BlockSpec((d, n), lambda bi, i: (0, 0)),
                  pl.BlockSpec((SUBLANES, conv_cols), lambda bi, i: (0, 0))],
        out_specs=pl.BlockSpec((1, tm, n), lambda bi, i: (bi, i, 0)),
        out_shape=jax.ShapeDtypeStruct((b, l, n), F32),
        scratch_shapes=[pltpu.VMEM((tm + 2 * SUBLANES, tn), F32)],
        compiler_params=_params("arbitrary", "arbitrary"),
        name="gdn_proj",
    )(x, xp, xn, scale, shift, w, conv_w)


def _gdn_kernel(q_ref, k_ref, v_ref, ab_ref, al_ref, dt_ref, s0_ref, o_ref, sf_ref, s_ref, qq_s, oc_s, ms_s, c0_s, gl_s,
                *, reverse, direction, heads):
    hg = pl.program_id(1)
    i = pl.program_id(2)
    tb = q_ref.shape[1]
    nc = tb // CHUNK
    per = s_ref.shape[0]

    @pl.when(i == 0)
    def _():
        s_ref[...] = s0_ref[0]
        qq_s[...] = jnp.zeros(qq_s.shape, F32)
        oc_s[...] = jnp.zeros(oc_s.shape, F32)
        ms_s[...] = jnp.zeros(ms_s.shape, F32)
        c0_s[...] = jnp.zeros(c0_s.shape, F32)
        gl_s[...] = jnp.ones(gl_s.shape, F32)

    state = [s_ref[hh] for hh in range(per)]
    pending = list(_chunk_order(nc, reverse))

    def tick():
        if pending:
            c = pending.pop(0)
            for hh in range(per):
                j = hh * nc + c
                o_ref[0, c * CHUNK:(c + 1) * CHUNK, hh * LANES:(hh + 1) * LANES] = _dot(qq_s[j], state[hh]) + oc_s[j]
                state[hh] = state[hh] * gl_s[j] - _dot(ms_s[j], state[hh]) + c0_s[j]

    def stack(x):
        return jnp.concatenate([_chunks(x[:, hh * LANES:(hh + 1) * LANES]) for hh in range(per)], axis=0)

    q, k, v = stack(q_ref[0]), stack(k_ref[0]), stack(v_ref[0])
    ab = ab_ref[0]
    beta_all = jax.nn.sigmoid(ab)
    g_all = -jnp.exp(al_ref[...]) * _softplus(ab + dt_ref[...])
    gc_all = _cumsum_chunks(g_all, reverse)
    lane = lax.broadcasted_iota(jnp.int32, (1, LANES), 1)

    def column(x, col):
        picked = jnp.sum(jnp.where(lane == col, x, 0.0), -1, keepdims=True)
        return _chunks(jnp.broadcast_to(picked, x.shape))

    betas, gs, gcs = [], [], []
    for hh in range(per):
        col = direction * 2 * heads + hg * per + hh
        betas.append(column(beta_all, col)[:, :, :1])
        gs.append(column(g_all, col + heads))
        gcs.append(column(gc_all, col + heads))
    beta = jnp.concatenate(betas, axis=0)
    g_b = jnp.concatenate(gs, axis=0)
    gc_b = jnp.concatenate(gcs, axis=0)

    incl, strict = _masks(reverse, per * nc)
    gtot = jnp.sum(g_b, axis=1, keepdims=True)
    m = gc_b[:, :, :CHUNK] - jnp.swapaxes(gc_b, 1, 2)[:, :CHUNK, :]
    decay = jnp.exp(jnp.where(incl, m, -jnp.inf))
    kb = k * beta
    lm = jnp.where(strict, _bdot_nt(kb, k) * decay, 0.0)
    tick()
    eg = jnp.exp(gc_b)
    sol = _tri_solve(jnp.concatenate([lm, jnp.zeros_like(lm)], axis=2),
                     jnp.concatenate([v * beta, kb * eg], axis=2), tick)
    u0, wk = sol[:, :, :LANES], sol[:, :, LANES:]
    tick()
    qk = _bdot_nt(q, k) * decay
    kd = k * jnp.exp(gtot - gc_b)
    gl = jnp.exp(gtot)
    ms = _bdot_tn(kd, wk)
    c0 = _bdot_tn(kd, u0)
    qq = q * eg - _bdot(qk, wk)
    oc = _bdot(qk, u0)

    while pending:
        tick()
    for hh in range(per):
        s_ref[hh] = state[hh]
        sf_ref[0, hh] = state[hh]
    qq_s[...] = qq
    oc_s[...] = oc
    ms_s[...] = ms
    c0_s[...] = c0
    gl_s[...] = gl


def _gdn_core(p, a_log, dt_bias, s0, *, reverse, direction, heads):
    b, l, _ = p.shape
    tb = _row_tile(l, SCAN_TILE)
    nblk = l // tb
    per = GDN_HEADS_PER_STEP
    assert heads % per == 0
    width = per * LANES
    nb = per * (tb // CHUNK)
    cur, prev = _lagged_blocks(nblk, reverse)

    def lane_row(t):
        full = jnp.zeros((2, 2, heads), F32).at[:, 1, :].set(t)
        return jnp.pad(full.reshape(1, 4 * heads), ((0, 0), (0, LANES - 4 * heads)))

    al, dt = lane_row(a_log), lane_row(dt_bias)

    def act(off):
        return pl.BlockSpec((1, tb, width), lambda bi, h, i: (bi, cur(i), off // per + h))

    per_head = pl.BlockSpec((1, LANES), lambda bi, h, i: (0, 0))
    state = pl.BlockSpec((1, per, LANES, LANES), lambda bi, h, i: (bi, h, 0, 0))
    ab_col = (3 * heads * LANES + heads * LANES) // LANES
    return pl.pallas_call(
        functools.partial(_gdn_kernel, reverse=reverse, direction=direction, heads=heads),
        grid=(b, heads // per, nblk + 1),
        in_specs=[act(0), act(heads), act(2 * heads),
                  pl.BlockSpec((1, tb, LANES), lambda bi, h, i: (bi, cur(i), ab_col)),
                  per_head, per_head, state],
        out_specs=[pl.BlockSpec((1, tb, width), lambda bi, h, i: (bi, prev(i), h)), state],
        out_shape=[jax.ShapeDtypeStruct((b, l, heads * LANES), F32),
                   jax.ShapeDtypeStruct(s0.shape, F32)],
        scratch_shapes=[pltpu.VMEM((per, LANES, LANES), F32),
                        pltpu.VMEM((nb, CHUNK, LANES), F32), pltpu.VMEM((nb, CHUNK, LANES), F32),
                        pltpu.VMEM((nb, LANES, LANES), F32), pltpu.VMEM((nb, LANES, LANES), F32),
                        pltpu.VMEM((nb, 1, LANES), F32)],
        compiler_params=_params("arbitrary", "arbitrary", "arbitrary"),
        name="gdn_bwd" if reverse else "gdn_fwd",
    )(p, p, p, p, al, dt, s0)


def _gla_kernel(q_ref, k_ref, v_ref, gr_ref, wg_ref, bg_ref, s0_ref, o_ref, sf_ref, s_ref, *, reverse):
    i = pl.program_id(2)
    tb = q_ref.shape[1]
    nc = tb // CHUNK
    per = s_ref.shape[0]
    dv = s_ref.shape[1]

    @pl.when(i == 0)
    def _():
        s_ref[...] = s0_ref[0]

    def stack(x, width):
        return jnp.concatenate([_chunks(x[:, hh * width:(hh + 1) * width]) for hh in range(per)], axis=0)

    q = stack(q_ref[0] * (LANES ** -0.5), LANES)
    k = stack(k_ref[0], LANES)
    v = stack(v_ref[0], dv)
    gr = gr_ref[0]
    gr_hi = gr.astype(BF16)
    gr_lo = (gr - gr_hi.astype(F32)).astype(BF16)
    gs, bcs = [], []
    for hh in range(per):
        gpre = (jnp.dot(gr_hi, wg_ref[hh, 0], preferred_element_type=F32)
                + jnp.dot(gr_lo, wg_ref[hh, 0], preferred_element_type=F32)
                + jnp.dot(gr_hi, wg_ref[hh, 1], preferred_element_type=F32)) + bg_ref[hh]
        g_h = -_softplus(-gpre) / GLA_TAU
        gs.append(_chunks(g_h))
        bcs.append(_chunks(_cumsum_chunks(g_h, reverse)))
    g = jnp.concatenate(gs, axis=0)
    bc = jnp.concatenate(bcs, axis=0)

    incl, _ = _masks(reverse, per * nc)
    btot = jnp.sum(g, axis=1, keepdims=True)
    mid = CHUNK - 1 - CHUNK // 2 if reverse else CHUNK // 2
    ref = bc[:, mid:mid + 1, :]
    att = jnp.where(incl, _bdot_nt(q * jnp.exp(bc - ref), k * jnp.exp(ref - bc)), 0.0)
    o_intra = _bdot(att, v)
    qg = q * jnp.exp(bc)
    kv = _bdot_tn(v, k * jnp.exp(btot - bc))
    gl = jnp.exp(btot)

    st = [s_ref[hh] for hh in range(per)]
    for c in _chunk_order(nc, reverse):
        for hh in range(per):
            j = hh * nc + c
            o_ref[0, c * CHUNK:(c + 1) * CHUNK, hh * dv:(hh + 1) * dv] = o_intra[j] + _dot_nt(qg[j], st[hh])
            st[hh] = st[hh] * gl[j] + kv[j]
    for hh in range(per):
        s_ref[hh] = st[hh]
        sf_ref[0, hh] = st[hh]


def _gla_core(p, wg, bg, s0, *, reverse, heads):
    b, l, _ = p.shape
    tb = _row_tile(l, SCAN_TILE)
    nblk = l // tb
    dk, dv = LANES, 2 * LANES
    per = GLA_HEADS_PER_STEP
    assert heads % per == 0
    groups = heads // per
    blk = (lambda i: nblk - 1 - i) if reverse else (lambda i: i)
    gr_col = (2 * heads * dk + 2 * heads * dv) // LANES
    state = pl.BlockSpec((1, per, dv, dk), lambda bi, h, i: (bi, h, 0, 0))
    return pl.pallas_call(
        functools.partial(_gla_kernel, reverse=reverse),
        grid=(b, groups, nblk),
        in_specs=[pl.BlockSpec((1, tb, per * dk), lambda bi, h, i: (bi, blk(i), h)),
                  pl.BlockSpec((1, tb, per * dk), lambda bi, h, i: (bi, blk(i), groups + h)),
                  pl.BlockSpec((1, tb, per * dv), lambda bi, h, i: (bi, blk(i), groups + h)),
                  pl.BlockSpec((1, tb, LANES), lambda bi, h, i: (bi, blk(i), gr_col)),
                  pl.BlockSpec((per, 2, LANES, dk), lambda bi, h, i: (h, 0, 0, 0)),
                  pl.BlockSpec((per, 1, dk), lambda bi, h, i: (h, 0, 0)),
                  state],
        out_specs=[pl.BlockSpec((1, tb, per * dv), lambda bi, h, i: (bi, blk(i), h)), state],
        out_shape=[jax.ShapeDtypeStruct((b, l, heads * dv), F32), jax.ShapeDtypeStruct(s0.shape, F32)],
        scratch_shapes=[pltpu.VMEM((per, dv, dk), F32)],
        compiler_params=_params("arbitrary", "arbitrary", "arbitrary"),
        name="gla_bwd" if reverse else "gla_fwd",
    )(p, p, p, p, wg, bg, s0)


def _rwkv_proj_kernel(x_ref, xp_ref, xn_ref, sc_ref, sh_ref, mu_ref, w4_ref, w1_ref, a1_ref, w2_ref, a2_ref,
                      w0_ref, a0_ref, r_ref, k_ref, v_ref, z_ref, lw_ref, a_ref):
    i = pl.program_id(1)
    last = pl.num_programs(1) - 1
    tm = x_ref.shape[1]
    rank = w2_ref.shape[1]
    sc, sh = sc_ref[0], sh_ref[0]
    u = _modulate(x_ref[0], sc, sh)
    up = jnp.where(i > 0, _modulate(xp_ref[0, 0, SUBLANES - 1:SUBLANES, :], sc, sh), 0.0)
    un = jnp.where(i < last, _modulate(xn_ref[0, 0, 0:1, :], sc, sh), 0.0)
    row = lax.broadcasted_iota(jnp.int32, (tm, 1), 0)
    prev = jnp.where(row == 0, up, pltpu.roll(u, 1, 0))
    nxt = jnp.where(row == tm - 1, un, pltpu.roll(u, tm - 1, 0))
    xx = 0.5 * (prev + nxt) - u

    def mix(j):
        return (u + xx * mu_ref[j:j + 1, :]).astype(BF16)

    r_ref[0] = jnp.dot(mix(0), w4_ref[0], preferred_element_type=F32)
    k_ref[0] = jnp.dot(mix(2), w4_ref[1], preferred_element_type=F32)
    v_ref[0] = jnp.dot(mix(3), w4_ref[2], preferred_element_type=F32)
    z_ref[0] = jnp.dot(mix(5), w4_ref[3], preferred_element_type=F32)
    lw1 = jnp.tanh(jnp.dot(mix(1), w1_ref[...], preferred_element_type=F32))
    a1 = jnp.dot(mix(4), a1_ref[...], preferred_element_type=F32)
    for n in range(2):
        wlog = w0_ref[n:n + 1, :] + _dot(lw1[:, n * rank:(n + 1) * rank], w2_ref[n])
        lw_ref[n, 0] = -jnp.exp(-_softplus(-wlog) - 0.5)
        a_ref[n, 0] = jax.nn.sigmoid(a0_ref[n:n + 1, :] + _dot(a1[:, n * rank:(n + 1) * rank], a2_ref[n]))


def _rwkv_proj(x, scale, shift, mu, w_rkvz, w0, w1, w2, a0, a1, a2):
    b, l, d = x.shape
    tm = _row_tile(l, 256)
    nblk = l // tm
    rank = w1.shape[2]
    x4 = x.reshape(b, nblk, tm, d)
    zeros = jnp.zeros((b, 1, SUBLANES, d), x.dtype)
    xp = jnp.concatenate([zeros, x4[:, :-1, tm - SUBLANES:, :]], axis=1)
    xn = jnp.concatenate([x4[:, 1:, :SUBLANES, :], zeros], axis=1)
    w1c = jnp.concatenate([w1[0], w1[1]], axis=1).astype(BF16)
    a1c = jnp.concatenate([a1[0], a1[1]], axis=1).astype(BF16)
    vec = pl.BlockSpec((1, 1, d), lambda bi, i: (bi, 0, 0))
    full = lambda a: pl.BlockSpec(a.shape, lambda bi, i: (0,) * a.ndim)
    halo = pl.BlockSpec((1, 1, SUBLANES, d), lambda bi, i: (bi, i, 0, 0))
    act = pl.BlockSpec((1, tm, d), lambda bi, i: (bi, i, 0))
    act2 = pl.BlockSpec((2, 1, tm, d), lambda bi, i: (0, bi, i, 0))
    w4 = w_rkvz.astype(BF16)
    w2b, a2b = w2.astype(BF16), a2.astype(BF16)
    sds = jax.ShapeDtypeStruct((b, l, d), F32)
    sds2 = jax.ShapeDtypeStruct((2, b, l, d), F32)
    return pl.pallas_call(
        _rwkv_proj_kernel,
        grid=(b, nblk),
        in_specs=[act, halo, halo, vec, vec, full(mu), full(w4), full(w1c), full(a1c), full(w2b), full(a2b),
                  full(w0), full(a0)],
        out_specs=[act, act, act, act, act2, act2],
        out_shape=[sds, sds, sds, sds, sds2, sds2],
        compiler_params=_params("arbitrary", "arbitrary"),
        name="rwkv_proj",
    )(x, xp, xn, scale, shift, mu, w4, w1c, a1c, w2b, a2b, w0, a0)


def _rwkv_kernel(r_ref, k_ref, v_ref, lw_ref, a_ref, kk_ref, ka_ref, rk_ref, s0_ref, y_ref, bv_ref, sf_ref, s_ref,
                 rq_s, yc_s, ms_s, c0_s, gl_s, *, reverse, n):
    i = pl.program_id(2)
    tb = r_ref.shape[1]
    nc = tb // CHUNK
    per = LANES // n

    @pl.when(i == 0)
    def _():
        s_ref[...] = s0_ref[0, 0]
        rq_s[...] = jnp.zeros(rq_s.shape, F32)
        yc_s[...] = jnp.zeros(yc_s.shape, F32)
        ms_s[...] = jnp.zeros(ms_s.shape, F32)
        c0_s[...] = jnp.zeros(c0_s.shape, F32)
        gl_s[...] = jnp.ones(gl_s.shape, F32)

    state = [s_ref[...]]
    pending = list(_chunk_order(nc, reverse))

    def tick():
        if pending:
            c = pending.pop(0)
            y_ref[0, c * CHUNK:(c + 1) * CHUNK, :] = _dot_nt(rq_s[c], state[0]) + yc_s[c]
            state[0] = state[0] * gl_s[c] - _dot(state[0], ms_s[c]) + c0_s[c]

    lane = lax.broadcasted_iota(jnp.int32, (1, LANES), 1)
    in_head = [lane // n == hh for hh in range(per)]

    def by_head(fn):
        out = fn(per - 1)
        for hh in range(per - 2, -1, -1):
            out = jnp.where(in_head[hh], fn(hh), out)
        return out

    def head_sum(x):
        return by_head(lambda hh: jnp.sum(jnp.where(in_head[hh], x, 0.0), -1, keepdims=True))

    r, kr, v, lw, a = r_ref[0], k_ref[0], v_ref[0], lw_ref[0, 0], a_ref[0, 0]
    kd = kr * (1.0 + (a - 1.0) * ka_ref[...])
    kq = kr * kk_ref[...]
    kk = kq * lax.rsqrt(head_sum(kq * kq) + RMS_EPS)
    kka = kk * a
    bv_ref[0] = head_sum(r * kd * rk_ref[...]) * v
    gc = _chunks(_cumsum_chunks(lw, reverse))
    r, kk, kka, kd, v, lw = _chunks(r), _chunks(kk), _chunks(kka), _chunks(kd), _chunks(v), _chunks(lw)
    gtot = jnp.sum(lw, axis=1, keepdims=True)
    gcx = gc - lw
    ref = 0.5 * gtot
    e_ref = jnp.exp(ref)
    gl = jnp.exp(gtot)
    at = kk * jnp.exp(gcx - ref)
    rt = r * jnp.exp(gc - ref)
    e_in = jnp.exp(ref - gc)
    bt = kka * e_in
    kt = kd * e_in
    a0 = at * e_ref
    bh = bt * e_ref

    def heads(x):
        return jnp.concatenate([x] * per, axis=0)

    def pick(x):
        return by_head(lambda hh: x[hh * nc:(hh + 1) * nc])

    lhs = jnp.concatenate([jnp.concatenate([jnp.where(in_head[hh], at, 0.0), jnp.where(in_head[hh], rt, 0.0)], axis=1)
                           for hh in range(per)], axis=0)
    rhs = jnp.concatenate([bt, kt], axis=1)
    row = lax.broadcasted_iota(jnp.int32, (1, 2 * CHUNK, 2 * CHUNK), 1)
    col = lax.broadcasted_iota(jnp.int32, (1, 2 * CHUNK, 2 * CHUNK), 2) & (CHUNK - 1)
    before = (col > row) if reverse else (col < row)
    upto = (col >= row - CHUNK) if reverse else (col <= row - CHUNK)
    keep = ((row < CHUNK) & before) | ((row >= CHUNK) & upto)
    aa = jnp.where(keep, _bdot_nt(lhs, heads(rhs)), 0.0)
    tick()
    top, bot = aa[:, :CHUNK, :], aa[:, CHUNK:, :]
    zeros = jnp.zeros_like(v)
    akv = _bdot(top, heads(jnp.concatenate([zeros, v], axis=1)))
    tick()
    sol = _tri_solve(top, jnp.concatenate([akv, heads(a0)], axis=2), tick)
    u0 = -pick(sol[:, :, :LANES])
    w = pick(sol[:, :, LANES:])
    ry = _bdot(bot, heads(jnp.concatenate([jnp.concatenate([w, zeros], axis=1),
                                           jnp.concatenate([u0, v], axis=1)], axis=2)))
    rq = rt * e_ref - pick(ry[:, :, :LANES])
    yc = pick(ry[:, :, LANES:])
    r2 = lax.broadcasted_iota(jnp.int32, (1, LANES, LANES), 1)
    c2 = lax.broadcasted_iota(jnp.int32, (1, LANES, LANES), 2)
    same_head = (r2 // n) == (c2 // n)
    ms = jnp.where(same_head, _bdot_tn(w, bh), 0.0)
    c0 = jnp.where(same_head, _bdot_tn(jnp.concatenate([u0, v], axis=1),
                                       jnp.concatenate([bh, kt * e_ref], axis=1)), 0.0)

    while pending:
        tick()
    s_ref[...] = state[0]
    sf_ref[0, 0] = state[0]
    rq_s[...] = rq
    yc_s[...] = yc
    ms_s[...] = ms
    c0_s[...] = c0
    gl_s[...] = gl


def _rwkv_core(r, k, v, lw, a, k_k, k_a, r_k, s0, *, reverse, direction):
    b, l, d = r.shape
    n = r_k.shape[-1]
    tb = _row_tile(l, SCAN_TILE)
    nblk = l // tb
    nc = tb // CHUNK
    cur, prev = _lagged_blocks(nblk, reverse)
    act = pl.BlockSpec((1, tb, LANES), lambda bi, h, i: (bi, cur(i), h))
    act2 = pl.BlockSpec((1, 1, tb, LANES), lambda bi, h, i: (direction, bi, cur(i), h))
    lagged = pl.BlockSpec((1, tb, LANES), lambda bi, h, i: (bi, prev(i), h))
    vec = pl.BlockSpec((1, LANES), lambda bi, h, i: (0, h))
    state = pl.BlockSpec((1, 1, LANES, LANES), lambda bi, h, i: (bi, h, 0, 0))
    sds = jax.ShapeDtypeStruct((b, l, d), F32)
    return pl.pallas_call(
        functools.partial(_rwkv_kernel, reverse=reverse, n=n),
        grid=(b, d // LANES, nblk + 1),
        in_specs=[act, act, act, act2, act2, vec, vec, vec, state],
        out_specs=[lagged, act, state],
        out_shape=[sds, sds, jax.ShapeDtypeStruct(s0.shape, F32)],
        scratch_shapes=[pltpu.VMEM((LANES, LANES), F32),
                        pltpu.VMEM((nc, CHUNK, LANES), F32), pltpu.VMEM((nc, CHUNK, LANES), F32),
                        pltpu.VMEM((nc, LANES, LANES), F32), pltpu.VMEM((nc, LANES, LANES), F32),
                        pltpu.VMEM((nc, 1, LANES), F32)],
        compiler_params=_params("arbitrary", "arbitrary", "arbitrary"),
        name="rwkv_bwd" if reverse else "rwkv_fwd",
    )(r, k, v, lw, a, k_k.reshape(1, d), k_a.reshape(1, d), r_k[direction].reshape(1, d), s0)


def _mla_proj_kernel(x_ref, sc_ref, sh_ref, win_ref, qn_ref, kvn_ref, wq_ref, wk_ref, wv_ref, cq_ref, s1_ref, s2_ref,
                     k_all_ref, v_all_ref, q_ref, k_ref, v_ref, z_ref, *, heads, q_lora, kv_lora, scale):
    del k_all_ref, v_all_ref
    d = z_ref.shape[2]
    h = _modulate(x_ref[0], sc_ref[0], sh_ref[0]).astype(BF16)
    p = jnp.dot(h, win_ref[...], preferred_element_type=F32)

    def rms(t, g):
        return t * lax.rsqrt(jnp.mean(t * t, -1, keepdims=True) + RMS_EPS) * g

    def rope(t):
        return (t * cq_ref[...] + pltpu.roll(t, LANES - LANES // 4, 1) * s1_ref[...]
                + pltpu.roll(t, LANES // 4, 1) * s2_ref[...])

    ql = rms(p[:, :q_lora], qn_ref[...]).astype(BF16)
    kvl = rms(p[:, q_lora:q_lora + kv_lora], kvn_ref[...]).astype(BF16)
    z_ref[0] = p[:, q_lora + kv_lora:q_lora + kv_lora + d]
    kr = rope(p[:, q_lora + kv_lora + d:])
    q = jnp.dot(ql, wq_ref[...], preferred_element_type=F32) * scale
    kn = jnp.dot(kvl, wk_ref[...], preferred_element_type=F32)
    v_ref[0] = jnp.dot(kvl, wv_ref[...], preferred_element_type=F32).astype(BF16)
    for j in range(heads):
        base = 2 * LANES * j
        q_ref[0, :, base:base + LANES] = q[:, base:base + LANES].astype(BF16)
        q_ref[0, :, base + LANES:base + 2 * LANES] = rope(q[:, base + LANES:base + 2 * LANES]).astype(BF16)
        k_ref[0, :, base:base + LANES] = kn[:, j * LANES:(j + 1) * LANES].astype(BF16)
        k_ref[0, :, base + LANES:base + 2 * LANES] = kr.astype(BF16)


def _mla_proj(x, scale, shift, k_all, v_all, row0, w_in, q_norm, kv_norm, w_uq, w_ukv, cos, sin,
              *, heads, nope, rope, dv):
    b, l, d = x.shape
    q_lora, kv_lora = q_norm.shape[0], kv_norm.shape[0]
    tm = _row_tile(l, 256)
    assert row0 % tm == 0
    half = rope // 2
    assert nope == LANES and dv == LANES and rope == LANES // 2
    o2 = q_lora + kv_lora
    w_in_r = jnp.concatenate([w_in[:, :o2], w_in[:, o2 + rope:], w_in[:, o2:o2 + rope],
                              jnp.zeros((d, LANES - rope), w_in.dtype)], axis=1).astype(BF16)
    wq = w_uq.reshape(q_lora, heads, nope + rope)
    wq = jnp.concatenate([wq, jnp.zeros((q_lora, heads, LANES - rope), w_uq.dtype)], axis=2)
    wq = wq.reshape(q_lora, heads * 2 * LANES).astype(BF16)
    wkv = w_ukv.reshape(kv_lora, heads, nope + dv)
    wk = wkv[:, :, :nope].reshape(kv_lora, heads * nope).astype(BF16)
    wv = wkv[:, :, nope:].reshape(kv_lora, heads * dv).astype(BF16)
    zer = jnp.zeros((l, half), F32)
    zer2 = jnp.zeros((l, LANES - rope), F32)
    cq = jnp.concatenate([cos, cos, zer2], axis=1)
    s1 = jnp.concatenate([-sin, zer, zer2], axis=1)
    s2 = jnp.concatenate([zer, sin, zer2], axis=1)
    vec = pl.BlockSpec((1, 1, d), lambda bi, i: (bi, 0, 0))
    full = lambda a: pl.BlockSpec(a.shape, lambda bi, i: (0,) * a.ndim)
    tab = pl.BlockSpec((tm, LANES), lambda bi, i: (i, 0))
    row = lambda w: pl.BlockSpec((1, tm, w), lambda bi, i: (bi, i, 0))
    row_at = lambda w: pl.BlockSpec((1, tm, w), lambda bi, i: (bi, row0 // tm + i, 0))
    in_place = pl.BlockSpec(memory_space=pl.ANY)
    qn = q_norm.reshape(1, q_lora)
    kvn = kv_norm.reshape(1, kv_lora)
    return pl.pallas_call(
        functools.partial(_mla_proj_kernel, heads=heads, q_lora=q_lora, kv_lora=kv_lora,
                          scale=(nope + rope) ** -0.5 * math.log2(math.e)),
        grid=(b, l // tm),
        in_specs=[row(d), vec, vec, full(w_in_r), full(qn), full(kvn), full(wq), full(wk), full(wv), tab, tab, tab,
                  in_place, in_place],
        out_specs=[row(heads * 2 * LANES), row_at(heads * 2 * LANES), row_at(heads * dv), row(d)],
        out_shape=[jax.ShapeDtypeStruct((b, l, heads * 2 * LANES), BF16),
                   jax.ShapeDtypeStruct(k_all.shape, BF16),
                   jax.ShapeDtypeStruct(v_all.shape, BF16),
                   jax.ShapeDtypeStruct((b, l, d), F32)],
        input_output_aliases={12: 1, 13: 2},
        compiler_params=_params("arbitrary", "arbitrary"),
        name="mla_proj",
    )(x, scale, shift, w_in_r, qn, kvn, wq, wk, wv, cq, s1, s2, k_all, v_all)


def _flash_kernel(q_ref, k_ref, v_ref, o_ref, m_ref, acc_ref):
    j = pl.program_id(3)

    @pl.when(j == 0)
    def _():
        m_ref[...] = jnp.full(m_ref.shape, -jnp.inf, F32)
        acc_ref[...] = jnp.zeros(acc_ref.shape, F32)

    groups = q_ref.shape[1] // FLASH_ROWS

    def scores(g):
        return lax.dot_general(q_ref[0, g * FLASH_ROWS:(g + 1) * FLASH_ROWS, :], k_ref[0],
                               (((1,), (1,)), ((), ())), preferred_element_type=F32)

    v1 = jnp.concatenate([v_ref[0], jnp.ones(v_ref.shape[1:], BF16)], axis=1)
    s_next = scores(0)
    for g in range(groups):
        rows = slice(g * FLASH_ROWS, (g + 1) * FLASH_ROWS)
        s = s_next
        if g + 1 < groups:
            s_next = scores(g + 1)
        m_old = m_ref[rows, :]
        m_new = jnp.maximum(m_old, jnp.max(s, -1, keepdims=True))
        alpha = jnp.exp2(m_old - m_new)
        p = jnp.exp2(s - m_new)
        acc_ref[rows, :] = alpha * acc_ref[rows, :] + jnp.dot(p.astype(BF16), v1, preferred_element_type=F32)
        m_ref[rows, :] = m_new

    @pl.when(j == pl.num_programs(3) - 1)
    def _():
        o_ref[0] = acc_ref[:, :LANES] / acc_ref[:, LANES:]


def _flash(q, k, v, *, heads):
    b, l, _ = q.shape
    lk = k.shape[1]
    tq = _row_tile(l, FLASH_Q)
    tk = LANES
    for t in range(LANES, FLASH_K + 1, LANES):
        if lk % t == 0:
            tk = t
    return pl.pallas_call(
        _flash_kernel,
        grid=(b, heads, l // tq, lk // tk),
        in_specs=[pl.BlockSpec((1, tq, 2 * LANES), lambda bi, h, i, j: (bi, i, h)),
                  pl.BlockSpec((1, tk, 2 * LANES), lambda bi, h, i, j: (bi, j, h)),
                  pl.BlockSpec((1, tk, LANES), lambda bi, h, i, j: (bi, j, h))],
        out_specs=pl.BlockSpec((1, tq, LANES), lambda bi, h, i, j: (bi, i, h)),
        out_shape=jax.ShapeDtypeStruct((b, l, heads * LANES), F32),
        scratch_shapes=[pltpu.VMEM((tq, 1), F32), pltpu.VMEM((tq, 2 * LANES), F32)],
        compiler_params=_params("arbitrary", "arbitrary", "arbitrary", "arbitrary"),
        name="flash",
    )(q, k, v)


def _pad_cols(w):
    n = w.shape[1]
    return jnp.pad(w, ((0, 0), (0, -n % LANES))).astype(BF16)


def _gdn_layer(x, xc, mod, mod_c, w_in, conv_w, a_log, dt_bias, norm_g, w_out, ln_g, ln_b, alpha, need_ctx):
    heads = a_log.shape[1]
    dk = dv = norm_g.shape[0]
    assert dk == LANES and conv_w.shape[1] == 3 * heads * dk
    b = x.shape[0]
    w = _pad_cols(w_in)
    zero = jnp.zeros((b, heads, dk, dv), F32)

    def run(xs, m, s_f, s_b):
        p = _gdn_proj(xs, m[1], m[0], w, conv_w, 3 * heads * dk)
        core = functools.partial(_gdn_core, p, a_log, dt_bias, heads=heads)
        o_f, s_f = core(s_f, reverse=False, direction=0)
        o_b, s_b = core(s_b, reverse=True, direction=1)
        return p, o_f, o_b, s_f, s_b

    def finish(xs, m, p, o_f, o_b):
        specs = lambda tm: [_cols(tm, heads * dv, 0), _cols(tm, heads * dv, 0), _cols(tm, heads * dv, 3)]
        return _out("rms", [o_f, o_b, p], specs, [norm_g.reshape(1, dv)], xs, m[2], w_out.astype(BF16),
                    ln_g, ln_b, alpha, width=dv)

    pc, oc_f, oc_b, s_f, s_b = run(xc, mod_c, zero, zero)
    p, o_f, o_b, _, _ = run(x, mod, s_f, s_b)
    x_new = finish(x, mod, p, o_f, o_b)
    xc_new = finish(xc, mod_c, pc, oc_f, oc_b) if need_ctx else None
    return x_new, xc_new


def _gla_layer(x, xc, mod, mod_c, w_in, w_g2, b_g, norm_g, w_out, ln_g, ln_b, alpha, need_ctx):
    dv = norm_g.shape[0]
    rank, qk = w_g2.shape[1], w_g2.shape[2]
    dk = LANES
    heads = qk // dk
    assert dv == 2 * LANES and w_in.shape[1] == 2 * qk + 2 * heads * dv + 2 * rank
    b = x.shape[0]
    w = _pad_cols(w_in)
    zero = jnp.zeros((b, heads, dv, dk), F32)

    def gate_w(direction):
        wg = jnp.zeros((LANES, qk), F32).at[direction * rank:(direction + 1) * rank].set(w_g2[direction])
        wg = wg.reshape(LANES, heads, dk).transpose(1, 0, 2)
        hi = wg.astype(BF16)
        lo = (wg - hi.astype(F32)).astype(BF16)
        return jnp.stack([hi, lo], axis=1), b_g[direction].reshape(heads, 1, dk)

    def run(xs, m, s_f, s_b):
        p = _proj(xs, m[1], m[0], w)
        o_f, s_f = _gla_core(p, *gate_w(0), s_f, reverse=False, heads=heads)
        o_b, s_b = _gla_core(p, *gate_w(1), s_b, reverse=True, heads=heads)
        return p, o_f, o_b, s_f, s_b

    def finish(xs, m, p, o_f, o_b):
        specs = lambda tm: [_cols(tm, heads * dv, 0), _cols(tm, heads * dv, 0), _cols(tm, heads * dv, 2)]
        return _out("rms", [o_f, o_b, p], specs, [norm_g.reshape(1, dv)], xs, m[2], w_out.astype(BF16),
                    ln_g, ln_b, alpha, width=dv)

    pc, oc_f, oc_b, s_f, s_b = run(xc, mod_c, zero, zero)
    p, o_f, o_b, _, _ = run(x, mod, s_f, s_b)
    x_new = finish(x, mod, p, o_f, o_b)
    xc_new = finish(xc, mod_c, pc, oc_f, oc_b) if need_ctx else None
    return x_new, xc_new


def _rwkv_layer(x, xc, mod, mod_c, mu, w_rkvz, w0, w1, w2, a0, a1, a2, k_k, k_a, r_k, gn_g, gn_b, w_out,
                ln_g, ln_b, alpha, need_ctx):
    b, _, d = x.shape
    heads, n = r_k.shape
    zero = jnp.zeros((b, d // LANES, LANES, LANES), F32)
    r_k2 = jnp.broadcast_to(r_k[None], (2, heads, n))

    def run(xs, m, s_f, s_b):
        r, k, v, z, lw, a = _rwkv_proj(xs, m[1], m[0], mu, w_rkvz, w0, w1, w2, a0, a1, a2)
        core = functools.partial(_rwkv_core, r, k, v, lw, a, k_k, k_a, r_k2)
        y_f, bv_f, s_f = core(s_f, reverse=False, direction=0)
        y_b, bv_b, s_b = core(s_b, reverse=True, direction=1)
        return (y_f, y_b, bv_f, bv_b, z), s_f, s_b

    def finish(xs, m, acts):
        specs = lambda tm: [_cols(tm, d, 0)] * 5
        return _out("rwkv", list(acts), specs, [gn_g.reshape(1, d), gn_b.reshape(1, d)], xs, m[2],
                    w_out.astype(BF16), ln_g, ln_b, alpha, width=n)

    acts_c, s_f, s_b = run(xc, mod_c, zero, zero)
    acts, _, _ = run(x, mod, s_f, s_b)
    x_new = finish(x, mod, acts)
    xc_new = finish(xc, mod_c, acts_c) if need_ctx else None
    return x_new, xc_new


def _rope_tables(n_tokens, rope):
    rows = n_tokens // GRID_W
    row = jnp.repeat(jnp.arange(rows, dtype=F32), GRID_W)
    col = jnp.tile(jnp.arange(GRID_W, dtype=F32), rows)
    n_freq = rope // 4
    inv_freq = ROPE_BASE ** (-jnp.arange(n_freq, dtype=F32) / n_freq)
    ang = jnp.concatenate([row[:, None] * inv_freq, col[:, None] * inv_freq], axis=-1)
    return jnp.cos(ang), jnp.sin(ang)


def _mla_layer(x, xc, mod, mod_c, w_in, q_norm, kv_norm, w_uq, w_ukv, w_out, ln_g, ln_b, alpha):
    b, l, d = x.shape
    lc = xc.shape[1]
    q_lora, kv_lora = q_norm.shape[0], kv_norm.shape[0]
    dv = LANES
    heads = w_out.shape[0] // dv
    rope = w_in.shape[1] - q_lora - kv_lora - heads * dv
    nope = w_uq.shape[1] // heads - rope
    cos, sin = _rope_tables(l, rope)
    proj = functools.partial(_mla_proj, w_in=w_in, q_norm=q_norm, kv_norm=kv_norm, w_uq=w_uq, w_ukv=w_ukv,
                             heads=heads, nope=nope, rope=rope, dv=dv)
    k_all = jnp.zeros((b, l + lc, heads * 2 * LANES), BF16)
    v_all = jnp.zeros((b, l + lc, heads * dv), BF16)
    q, k_all, v_all, z = proj(x, mod[1], mod[0], k_all, v_all, 0, cos=cos, sin=sin)
    ones, zeros = jnp.ones((lc, rope // 2), F32), jnp.zeros((lc, rope // 2), F32)
    _, k_all, v_all, _ = proj(xc, mod_c[1], mod_c[0], k_all, v_all, l, cos=ones, sin=zeros)
    o = _flash(q, k_all, v_all, heads=heads)
    specs = lambda tm: [_cols(tm, d, 0), _cols(tm, d, 0)]
    return _out("mla", [o, z], specs, [], x, mod[2], w_out.astype(BF16), ln_g, ln_b, alpha)


def kernel(x, c, ctx, c_ctx, ada_w, ada_b, ln_g, ln_b, gdn_w_in, gdn_conv, gdn_a_log, gdn_dt_bias, gdn_norm, gdn_w_out, rwkv_mu, rwkv_w_rkvz, rwkv_w0, rwkv_w1, rwkv_w2, rwkv_a0, rwkv_a1, rwkv_a2, rwkv_k_k, rwkv_k_a, rwkv_r_k, rwkv_gn_g, rwkv_gn_b, rwkv_w_out, gla_w_in, gla_w_g2, gla_b_g, gla_norm, gla_w_out, mla_w_in, mla_q_norm, mla_kv_norm, mla_w_uq, mla_w_ukv, mla_w_out):
    b, _, d = x.shape
    depth = ada_w.shape[0]
    n_mixers = 4
    assert depth == n_mixers, "one layer of each mixer; the last (MLA) layer needs no context output"
    alpha = (2.0 * depth) ** 0.25

    rows = -(-(b + 1) // SUBLANES) * SUBLANES
    cvec = jnp.concatenate([c, c_ctx[None], jnp.zeros((rows - b - 1, d), F32)], axis=0)
    mods = _ada(cvec, ada_w, ada_b)

    def split(i):
        lat = [mods[i, :b, j * d:(j + 1) * d].reshape(b, 1, d) for j in range(3)]
        con = [jnp.broadcast_to(mods[i, b, j * d:(j + 1) * d].reshape(1, 1, d), (b, 1, d)) for j in range(3)]
        return lat, con

    xc = ctx
    m, mc = split(0)
    x, xc = _gdn_layer(x, xc, m, mc, gdn_w_in[0], gdn_conv[0], gdn_a_log[0], gdn_dt_bias[0], gdn_norm[0],
                       gdn_w_out[0], ln_g[0], ln_b[0], alpha, True)
    m, mc = split(1)
    x, xc = _rwkv_layer(x, xc, m, mc, rwkv_mu[0], rwkv_w_rkvz[0], rwkv_w0[0], rwkv_w1[0], rwkv_w2[0], rwkv_a0[0],
                        rwkv_a1[0], rwkv_a2[0], rwkv_k_k[0], rwkv_k_a[0], rwkv_r_k[0], rwkv_gn_g[0], rwkv_gn_b[0],
                        rwkv_w_out[0], ln_g[1], ln_b[1], alpha, True)
    m, mc = split(2)
    x, xc = _gla_layer(x, xc, m, mc, gla_w_in[0], gla_w_g2[0], gla_b_g[0], gla_norm[0], gla_w_out[0],
                       ln_g[2], ln_b[2], alpha, True)
    m, mc = split(3)
    return _mla_layer(x, xc, m, mc, mla_w_in[0], mla_q_norm[0], mla_kv_norm[0], mla_w_uq[0], mla_w_ukv[0],
                      mla_w_out[0], ln_g[3], ln_b[3], alpha)
```
